```python
import math
import numpy as np
import jax
import jax.numpy as jnp
from jax import lax

D_MODEL = 1024
BATCH = 2
SEQ = 8192
DEPTH = 2

GRID_W = 64
CTX_LEN = 256
DN_HEADS = 8
DN_DK = 64
DN_DV = 64
DN_CONV = 5
DN_CHUNK = 64
SWA_Q_HEADS = 8
SWA_KV_HEADS = 2
SWA_HD = 64
SWA_WINDOW = 128
SWA_BLOCK = 128
ROPE_THETA = 10000.0
ROPE_FREQS = SWA_HD // 4
DN_QK = DN_HEADS * DN_DK
D_DN = DN_HEADS * DN_DV
D_SWA = SWA_Q_HEADS * SWA_HD
D_KV = SWA_KV_HEADS * SWA_HD
D_MIX = D_DN + D_SWA
DN_CONV_CH = 2 * DN_QK + D_DN
D_IN = 2 * DN_QK + 2 * D_DN + 4 * DN_HEADS + D_SWA + 2 * D_KV
N_EXPERTS = 32
TOP_K = 4
D_EXPERT = D_MODEL
SWIGLU_LIMIT = 7.0
SWIGLU_ALPHA = 1.702
MOE_BLOCK = 128
NORM_EPS = 1e-6
NEG_INF = -1e30

kernel_name = "hybrid_deltanet_swa_moe_dit"


def rmsnorm(x, g):
    xf = x.astype(jnp.float32)
    y = xf * lax.rsqrt(jnp.mean(xf * xf, axis=-1, keepdims=True) + NORM_EPS)
    return (y * g.astype(jnp.float32)).astype(x.dtype)


def l2norm(x):
    xf = x.astype(jnp.float32)
    return (xf * lax.rsqrt(jnp.sum(xf * xf, axis=-1, keepdims=True) + NORM_EPS)).astype(x.dtype)


def modulate(x, g, shift, scale):
    return rmsnorm(x, g) * (1.0 + scale) + shift


def split_proj(p):
    sizes = (DN_QK, DN_QK, D_DN, D_DN, 2 * DN_HEADS, 2 * DN_HEADS, D_SWA, D_KV, D_KV)
    idx = np.cumsum(np.array(sizes))[:-1].tolist()
    return jnp.split(p, idx, axis=-1)


def axial_rope_tables(n_tokens):
    rows = n_tokens // GRID_W
    row = jnp.repeat(jnp.arange(rows, dtype=jnp.float32), GRID_W)
    col = jnp.tile(jnp.arange(GRID_W, dtype=jnp.float32), rows)
    inv_freq = jnp.power(ROPE_THETA, -jnp.arange(ROPE_FREQS, dtype=jnp.float32) / ROPE_FREQS)
    ang_r = row[:, None] * inv_freq
    ang_c = col[:, None] * inv_freq
    return (jnp.cos(ang_r), jnp.sin(ang_r), jnp.cos(ang_c), jnp.sin(ang_c))


def rope_rotate(x, cos, sin):
    x1, x2 = jnp.split(x, 2, axis=-1)
    cos = cos[:, None, :].astype(x.dtype)
    sin = sin[:, None, :].astype(x.dtype)
    return jnp.concatenate([x1 * cos - x2 * sin, x2 * cos + x1 * sin], axis=-1)


def axial_rope(x, tabs):
    cos_r, sin_r, cos_c, sin_c = tabs
    x_row, x_col = jnp.split(x, 2, axis=-1)
    return jnp.concatenate([rope_rotate(x_row, cos_r, sin_r), rope_rotate(x_col, cos_c, sin_c)], axis=-1)


def short_conv(x, w):
    pad = DN_CONV // 2
    y = lax.conv_general_dilated(x, w[:, None, :].astype(x.dtype), window_strides=(1,), padding=[(pad, pad)],
                                 dimension_numbers=('NWC', 'WIO', 'NWC'), feature_group_count=x.shape[-1])
    return jax.nn.silu(y)


def dn_prepare(p_q, p_k, p_v, p_a, p_b, conv_w, a_log, dt_bias):
    B, T, _ = p_q.shape
    qkv = short_conv(jnp.concatenate([p_q, p_k, p_v], axis=-1), conv_w)
    q, k, v = jnp.split(qkv, [DN_QK, 2 * DN_QK], axis=-1)
    q = l2norm(q.reshape(B, T, DN_HEADS, DN_DK)) * (DN_DK ** -0.5)
    k = l2norm(k.reshape(B, T, DN_HEADS, DN_DK))
    v = v.reshape(B, T, DN_HEADS, DN_DV)
    a = p_a.reshape(B, T, 2, DN_HEADS).astype(jnp.float32)
    b = p_b.reshape(B, T, 2, DN_HEADS).astype(jnp.float32)
    g = -jnp.exp(a_log.astype(jnp.float32)) * jax.nn.softplus(a + dt_bias.astype(jnp.float32))
    beta = jax.nn.sigmoid(b)
    return q, k, v, g, beta


def gated_delta_chunked(q, k, v, g, beta, s0):
    B, T, H, DK = q.shape
    DV = v.shape[-1]
    C = DN_CHUNK
    N = T // C
    f32 = jnp.float32

    def chunks(t):
        return t.astype(f32).reshape(B, N, C, H, -1).transpose(0, 3, 1, 2, 4)

    qc, kc, vc = chunks(q), chunks(k), chunks(v)
    gch = chunks(g[..., None])[..., 0]
    bch = chunks(beta[..., None])[..., 0]
    gcum = jnp.cumsum(gch, axis=-1)
    tri = jnp.tril(jnp.ones((C, C), bool))
    strict = jnp.tril(jnp.ones((C, C), bool), -1)
    decay = jnp.where(tri, jnp.exp(jnp.where(tri, gcum[..., :, None] - gcum[..., None, :], 0.0)), 0.0)
    kb = kc * bch[..., None]
    vb = vc * bch[..., None]
    eye = jnp.eye(C, dtype=f32)
    a_mat = eye + jnp.where(strict, jnp.einsum('bhnid,bhnjd->bhnij', kb, kc) * decay, 0.0)
    t_inv = lax.linalg.triangular_solve(a_mat, jnp.broadcast_to(eye, a_mat.shape), left_side=True,
                                        lower=True, unit_diagonal=True)
    eg = jnp.exp(gcum)
    u = t_inv @ vb
    w = t_inv @ (kb * eg[..., None])
    a_qk = jnp.einsum('bhnid,bhnjd->bhnij', qc, kc) * decay
    qg = qc * eg[..., None]
    kd = kc * jnp.exp(gcum[..., -1:] - gcum)[..., None]
    g_last = eg[..., -1]

    def step(state, xs):
        a_i, u_i, w_i, qg_i, kd_i, gl_i = xs
        v_new = u_i - w_i @ state
        o_i = qg_i @ state + a_i @ v_new
        state = state * gl_i[..., None, None] + jnp.einsum('bhck,bhcv->bhkv', kd_i, v_new)
        return state, o_i

    to_front = lambda t: jnp.moveaxis(t, 2, 0)
    s_final, o = lax.scan(step, s0.astype(f32), (to_front(a_qk), to_front(u), to_front(w), to_front(qg),
                                                  to_front(kd), to_front(g_last)))
    o = jnp.moveaxis(o, 0, 2).transpose(0, 2, 3, 1, 4).reshape(B, T, H, DV)
    return o.astype(v.dtype), s_final


def dn_bidirectional(q, k, v, g, beta, s0_fwd, s0_bwd):
    flip = lambda t: jnp.flip(t, axis=1)
    o_f, s_f = gated_delta_chunked(q, k, v, g[:, :, 0], beta[:, :, 0], s0_fwd)
    o_b, s_b = gated_delta_chunked(flip(q), flip(k), flip(v), flip(g[:, :, 1]), flip(beta[:, :, 1]), s0_bwd)
    return o_f + flip(o_b), s_f, s_b


def dn_gated_out(o, z, g):
    B, T = o.shape[:2]
    z = z.reshape(B, T, DN_HEADS, DN_DV)
    return (rmsnorm(o, g) * jax.nn.silu(z)).reshape(B, T, D_DN)


def swa_latent(q, k, v, kc, vc, sinks):
    B, S = q.shape[:2]
    BL = SWA_BLOCK
    NB = S // BL
    G = SWA_Q_HEADS // SWA_KV_HEADS
    scale = SWA_HD ** -0.5
    qb = q.reshape(B, NB, BL, SWA_KV_HEADS, G, SWA_HD)

    def band(t):
        tp = jnp.pad(t, ((0, 0), (BL, BL), (0, 0), (0, 0))).reshape(B, NB + 2, BL, SWA_KV_HEADS, SWA_HD)
        return jnp.concatenate([tp[:, :-2], tp[:, 1:-1], tp[:, 2:]], axis=2)

    kb, vb = band(k), band(v)
    s_loc = jnp.einsum('bnqhgd,bnkhd->bnhgqk', qb, kb).astype(jnp.float32) * scale
    rel = jnp.arange(3 * BL)[None, :] - jnp.arange(BL)[:, None]
    in_window = (rel >= BL - SWA_WINDOW) & (rel <= BL + SWA_WINDOW)
    key_pos = jnp.arange(NB)[:, None] * BL - BL + jnp.arange(3 * BL)[None, :]
    in_range = (key_pos >= 0) & (key_pos < S)
    mask = in_window[None] & in_range[:, None, :]
    s_loc = jnp.where(mask[None, :, None, None], s_loc, NEG_INF)
    s_ctx = jnp.einsum('bnqhgd,bkhd->bnhgqk', qb, kc).astype(jnp.float32) * scale
    sink = jnp.broadcast_to(sinks.astype(jnp.float32).reshape(SWA_KV_HEADS, G)[None, None, :, :, None, None],
                            s_loc.shape[:-1] + (1,))
    p = jax.nn.softmax(jnp.concatenate([s_loc, s_ctx, sink], axis=-1), axis=-1)
    n_loc = 3 * BL
    n_ctx = kc.shape[1]
    p_loc = p[..., :n_loc].astype(v.dtype)
    p_ctx = p[..., n_loc:n_loc + n_ctx].astype(v.dtype)
    o = jnp.einsum('bnhgqk,bnkhd->bnqhgd', p_loc, vb) + jnp.einsum('bnhgqk,bkhd->bnqhgd', p_ctx, vc)
    return o.reshape(B, S, D_SWA)


def swa_context(qc, kc, vc, sinks):
    B, CL = qc.shape[:2]
    G = SWA_Q_HEADS // SWA_KV_HEADS
    qg = qc.reshape(B, CL, SWA_KV_HEADS, G, SWA_HD)
    s = jnp.einsum('bqhgd,bkhd->bhgqk', qg, kc).astype(jnp.float32) * (SWA_HD ** -0.5)
    sink = jnp.broadcast_to(sinks.astype(jnp.float32).reshape(SWA_KV_HEADS, G)[None, :, :, None, None],
                            s.shape[:-1] + (1,))
    p = jax.nn.softmax(jnp.concatenate([s, sink], axis=-1), axis=-1)[..., :CL].astype(vc.dtype)
    o = jnp.einsum('bhgqk,bkhd->bqhgd', p, vc)
    return o.reshape(B, CL, D_SWA)


def moe(h, router_w, router_b, w_gate_up, b_gate_up, w_down, b_down):
    T, D = h.shape
    logits = (h @ router_w + router_b).astype(jnp.float32)
    top_val, top_idx = lax.top_k(logits, TOP_K)
    gates = jax.nn.softmax(top_val, axis=-1)
    n_assign = T * TOP_K
    flat_e = top_idx.reshape(-1)
    order = jnp.argsort(flat_e)
    e_sorted = flat_e[order]
    tok_sorted = (order // TOP_K).astype(jnp.int32)
    gate_sorted = gates.reshape(-1)[order]
    counts = jnp.bincount(flat_e, length=N_EXPERTS)
    start = jnp.cumsum(counts) - counts
    padded = (counts + MOE_BLOCK - 1) // MOE_BLOCK * MOE_BLOCK
    pad_end = jnp.cumsum(padded)
    pad_start = pad_end - padded
    dest = pad_start[e_sorted] + (jnp.arange(n_assign) - start[e_sorted])
    n_blocks = (n_assign + MOE_BLOCK - 1) // MOE_BLOCK + N_EXPERTS
    n_pad = n_blocks * MOE_BLOCK
    buf_tok = jnp.full((n_pad,), T, jnp.int32).at[dest].set(tok_sorted)
    buf_gate = jnp.zeros((n_pad,), jnp.float32).at[dest].set(gate_sorted)
    block_expert = jnp.clip(jnp.searchsorted(pad_end, jnp.arange(n_blocks) * MOE_BLOCK, side='right'),
                            0, N_EXPERTS - 1)
    h_pad = jnp.concatenate([h, jnp.zeros((1, D), h.dtype)], axis=0)
    xb = h_pad[buf_tok].reshape(n_blocks, MOE_BLOCK, D)

    def expert_block(args):
        xe, e = args
        gu = xe @ w_gate_up[e] + b_gate_up[e]
        gate, up = jnp.split(gu, 2, axis=-1)
        gate = jnp.minimum(gate, SWIGLU_LIMIT)
        up = jnp.clip(up, -SWIGLU_LIMIT, SWIGLU_LIMIT)
        act = (up + 1.0) * gate * jax.nn.sigmoid(SWIGLU_ALPHA * gate)
        return act @ w_down[e] + b_down[e]

    yb = lax.map(expert_block, (xb, block_expert))
    y = yb.reshape(n_pad, D) * buf_gate[:, None].astype(h.dtype)
    return jax.ops.segment_sum(y, buf_tok, num_segments=T + 1)[:T]


def hybrid_layer(x, xc, c, c_ctx, rope_tabs, ada_w, ada_b, norm1_g, w_in, dn_conv_w, dn_a_log, dn_dt_bias,
                 dn_out_g, q_norm_g, k_norm_g, sinks, w_out, norm2_g, router_w, router_b,
                 w_gate_up, b_gate_up, w_down, b_down, last):
    B, S, D = x.shape
    CL = xc.shape[1]
    mod_x = (jax.nn.silu(c) @ ada_w + ada_b)[:, None, :]
    mod_c = jax.nn.silu(c_ctx) @ ada_w + ada_b
    sh1, sc1, gt1, sh2, sc2, gt2 = jnp.split(mod_x, 6, axis=-1)
    csh1, csc1, cgt1, csh2, csc2, cgt2 = jnp.split(mod_c, 6, axis=-1)

    hx = modulate(x, norm1_g, sh1, sc1)
    hc = modulate(xc, norm1_g, csh1, csc1)
    xq, xk, xv, xz, xa, xb, xsq, xsk, xsv = split_proj(hx @ w_in)
    cq, ck, cv, cz, ca, cb, csq, csk, csv = split_proj(hc @ w_in)

    dn_c = dn_prepare(cq, ck, cv, ca, cb, dn_conv_w, dn_a_log, dn_dt_bias)
    dn_x = dn_prepare(xq, xk, xv, xa, xb, dn_conv_w, dn_a_log, dn_dt_bias)
    s0 = jnp.zeros((B, DN_HEADS, DN_DK, DN_DV), jnp.float32)
    o_dn_c, s_fwd, s_bwd = dn_bidirectional(*dn_c, s0, s0)
    o_dn_x, _, _ = dn_bidirectional(*dn_x, s_fwd, s_bwd)
    o_dn_x = dn_gated_out(o_dn_x, xz, dn_out_g)

    qx = axial_rope(rmsnorm(xsq.reshape(B, S, SWA_Q_HEADS, SWA_HD), q_norm_g), rope_tabs)
    kx = axial_rope(rmsnorm(xsk.reshape(B, S, SWA_KV_HEADS, SWA_HD), k_norm_g), rope_tabs)
    vx = xsv.reshape(B, S, SWA_KV_HEADS, SWA_HD)
    kc = rmsnorm(csk.reshape(B, CL, SWA_KV_HEADS, SWA_HD), k_norm_g)
    vc = csv.reshape(B, CL, SWA_KV_HEADS, SWA_HD)
    o_sw_x = swa_latent(qx, kx, vx, kc, vc, sinks)

    x = x + gt1 * (jnp.concatenate([o_dn_x, o_sw_x], axis=-1) @ w_out)
    if not last:
        qc = rmsnorm(csq.reshape(B, CL, SWA_Q_HEADS, SWA_HD), q_norm_g)
        o_sw_c = swa_context(qc, kc, vc, sinks)
        o_dn_c = dn_gated_out(o_dn_c, cz, dn_out_g)
        xc = xc + cgt1 * (jnp.concatenate([o_dn_c, o_sw_c], axis=-1) @ w_out)

    hx2 = modulate(x, norm2_g, sh2, sc2)
    if last:
        y = moe(hx2.reshape(B * S, D), router_w, router_b, w_gate_up, b_gate_up, w_down, b_down)
        x = x + gt2 * y.reshape(B, S, D)
    else:
        hc2 = modulate(xc, norm2_g, csh2, csc2)
        tokens = jnp.concatenate([hx2.reshape(B * S, D), hc2.reshape(B * CL, D)], axis=0)
        y = moe(tokens, router_w, router_b, w_gate_up, b_gate_up, w_down, b_down)
        x = x + gt2 * y[:B * S].reshape(B, S, D)
        xc = xc + cgt2 * y[B * S:].reshape(B, CL, D)
    return x, xc


def setup_inputs(seed: int = 0) -> dict:
    key = jax.random.key(seed)
    ks = jax.random.split(key, 24)
    f32 = jnp.float32
    L = DEPTH

    def nrm(k, shape, s):
        return jax.random.normal(k, shape, f32) * s

    dt = jnp.exp(jax.random.uniform(ks[10], (L, 2, DN_HEADS), f32, math.log(1e-3), math.log(1e-1)))
    return {
        "x": nrm(ks[0], (BATCH, SEQ, D_MODEL), 1.0),
        "c": nrm(ks[1], (BATCH, D_MODEL), 1.0),
        "ctx": nrm(ks[2], (BATCH, CTX_LEN, D_MODEL), 1.0),
        "c_ctx": nrm(ks[3], (D_MODEL,), 1.0),
        "ada_w": nrm(ks[4], (L, D_MODEL, 6 * D_MODEL), 0.5 * D_MODEL ** -0.5),
        "ada_b": nrm(ks[5], (L, 6 * D_MODEL), 0.01),
        "norm1_g": 1.0 + nrm(ks[6], (L, D_MODEL), 0.02),
        "w_in": nrm(ks[7], (L, D_MODEL, D_IN), D_MODEL ** -0.5),
        "dn_conv_w": nrm(ks[8], (L, DN_CONV, DN_CONV_CH), DN_CONV ** -0.5),
        "dn_a_log": jnp.log(jax.random.uniform(ks[9], (L, 2, DN_HEADS), f32, 1.0, 16.0)),
        "dn_dt_bias": jnp.log(jnp.expm1(dt)),
        "dn_out_g": 1.0 + nrm(ks[11], (L, DN_DV), 0.02),
        "q_norm_g": 1.0 + nrm(ks[12], (L, SWA_HD), 0.02),
        "k_norm_g": 1.0 + nrm(ks[13], (L, SWA_HD), 0.02),
        "sinks": nrm(ks[14], (L, SWA_Q_HEADS), 0.5),
        "w_out": nrm(ks[15], (L, D_MIX, D_MODEL), D_MIX ** -0.5),
        "norm2_g": 1.0 + nrm(ks[16], (L, D_MODEL), 0.02),
        "router_w": nrm(ks[17], (L, D_MODEL, N_EXPERTS), D_MODEL ** -0.5),
        "router_b": nrm(ks[18], (L, N_EXPERTS), 0.01),
        "w_gate_up": nrm(ks[19], (L, N_EXPERTS, D_MODEL, 2 * D_EXPERT), D_MODEL ** -0.5),
        "b_gate_up": nrm(ks[20], (L, N_EXPERTS, 2 * D_EXPERT), 0.01),
        "w_down": nrm(ks[21], (L, N_EXPERTS, D_EXPERT, D_MODEL), D_EXPERT ** -0.5),
        "b_down": nrm(ks[22], (L, N_EXPERTS, D_MODEL), 0.01),
    }


def reference(x, c, ctx, c_ctx, ada_w, ada_b, norm1_g, w_in, dn_conv_w, dn_a_log, dn_dt_bias, dn_out_g,
              q_norm_g, k_norm_g, sinks, w_out, norm2_g, router_w, router_b, w_gate_up, b_gate_up,
              w_down, b_down):
    rope_tabs = axial_rope_tables(x.shape[1])
    xc = ctx
    for l in range(DEPTH):
        x, xc = hybrid_layer(x, xc, c, c_ctx, rope_tabs, ada_w[l], ada_b[l], norm1_g[l], w_in[l], dn_conv_w[l],
                             dn_a_log[l], dn_dt_bias[l], dn_out_g[l], q_norm_g[l], k_norm_g[l], sinks[l],
                             w_out[l], norm2_g[l], router_w[l], router_b[l], w_gate_up[l], b_gate_up[l],
                             w_down[l], b_down[l], last=(l == DEPTH - 1))
    return x
```

```python
import functools
import math

import jax
import jax.numpy as jnp
import numpy as np
from jax import lax
from jax.experimental import pallas as pl
from jax.experimental.pallas import tpu as pltpu

F32 = jnp.float32
BF16 = jnp.bfloat16
HIGHEST = lax.Precision.HIGHEST

GRID_W = 64
DN_HEADS = 8
DN_DK = 64
DN_DV = 64
DN_CONV = 5
DN_CHUNK = 64
SWA_Q_HEADS = 8
SWA_KV_HEADS = 2
SWA_HD = 64
SWA_WINDOW = 128
SWA_BLOCK = 128
ROPE_THETA = 10000.0
ROPE_FREQS = SWA_HD // 4
DN_QK = DN_HEADS * DN_DK
D_DN = DN_HEADS * DN_DV
D_SWA = SWA_Q_HEADS * SWA_HD
D_KV = SWA_KV_HEADS * SWA_HD
N_EXPERTS = 32
TOP_K = 4
SWIGLU_LIMIT = 7.0
SWIGLU_ALPHA = 1.702
NORM_EPS = 1e-6
NEG_INF = -1e30

LANES = 128
SUBLANES = 8
TOK_TILE = 256
MOE_TILE = 256
VMEM_LIMIT = 48 * 1024 * 1024


def _sigmoid(x):
    return 1.0 / (1.0 + jnp.exp(-x))


def _cparams(sem, vmem=None):
    return pltpu.CompilerParams(dimension_semantics=sem, vmem_limit_bytes=vmem or VMEM_LIMIT)


def _head_sum(x2, bd):
    hi = x2.astype(BF16)
    lo = (x2 - hi.astype(F32)).astype(BF16)
    return (jnp.dot(hi, bd, preferred_element_type=F32) + jnp.dot(lo, bd, preferred_element_type=F32))


def _mod_chunk(mod_ref, b, n_batch, is_ctx, k, d):
    mb = mod_ref[pl.ds(b, 1), k * d:(k + 1) * d]
    mc = mod_ref[n_batch:n_batch + 1, k * d:(k + 1) * d]
    return jnp.where(is_ctx, mc, mb)


def _ada_body(c_ref, w_ref, b_ref, o_ref):
    c = c_ref[...]
    s = c * _sigmoid(c)
    o_ref[...] = jnp.dot(s, w_ref[...], precision=HIGHEST, preferred_element_type=F32) + b_ref[...]


def _ada(cvec, ada_w, ada_b):
    d, n = ada_w.shape
    tn = n // 4
    return pl.pallas_call(
        _ada_body,
        grid=(n // tn,),
        in_specs=[pl.BlockSpec((SUBLANES, d), lambda j: (0, 0)),
                  pl.BlockSpec((d, tn), lambda j: (0, j)),
                  pl.BlockSpec((1, tn), lambda j: (0, j))],
        out_specs=pl.BlockSpec((SUBLANES, tn), lambda j: (0, j)),
        out_shape=jax.ShapeDtypeStruct((SUBLANES, n), F32),
        compiler_params=_cparams(("arbitrary",)),
        name="ada_mod",
    )(cvec, ada_w, ada_b.reshape(1, n))


def _inproj_body(x_ref, mod_ref, g_ref, w_ref, oqkv, oz, osq, oskv, oab, *, tm, ctx, n_batch, d):
    b = pl.program_id(0)
    i = pl.program_id(1)
    x = x_ref[0]
    ms = jnp.mean(x * x, axis=-1, keepdims=True)
    y = x * lax.rsqrt(ms + NORM_EPS) * g_ref[...]
    rows = i * tm + lax.broadcasted_iota(jnp.int32, (tm, 1), 0)
    is_ctx = rows < ctx
    sh = _mod_chunk(mod_ref, b, n_batch, is_ctx, 0, d)
    sc = _mod_chunk(mod_ref, b, n_batch, is_ctx, 1, d)
    h = (y * (1.0 + sc) + sh).astype(BF16)
    p = jnp.dot(h, w_ref[...], preferred_element_type=F32)
    c0 = 2 * DN_QK + D_DN
    oqkv[0] = p[:, :c0]
    oz[0] = p[:, c0:c0 + D_DN]
    c1 = c0 + D_DN
    osq[0] = p[:, c1:c1 + D_SWA]
    c2 = c1 + D_SWA
    oskv[0] = p[:, c2:c2 + 2 * D_KV]
    c3 = c2 + 2 * D_KV
    oab[0] = p[:, c3:c3 + LANES]


def _inproj(xu, mod, g1, w_pad, ctx):
    n_batch, t, d = xu.shape
    tm = TOK_TILE
    npad = w_pad.shape[1]
    widths = (2 * DN_QK + D_DN, D_DN, D_SWA, 2 * D_KV, LANES)
    body = functools.partial(_inproj_body, tm=tm, ctx=ctx, n_batch=n_batch, d=d)
    return pl.pallas_call(
        body,
        grid=(n_batch, t // tm),
        in_specs=[pl.BlockSpec((1, tm, d), lambda b, i: (b, i, 0)),
                  pl.BlockSpec((SUBLANES, 6 * d), lambda b, i: (0, 0)),
                  pl.BlockSpec((1, d), lambda b, i: (0, 0)),
                  pl.BlockSpec((d, npad), lambda b, i: (0, 0))],
        out_specs=[pl.BlockSpec((1, tm, w), lambda b, i: (b, i, 0)) for w in widths],
        out_shape=[jax.ShapeDtypeStruct((n_batch, t, w), F32) for w in widths],
        compiler_params=_cparams(("parallel", "parallel")),
        name="inproj",
    )(xu, mod, g1.reshape(1, d), w_pad)


def _swap16(x):
    lane = lax.broadcasted_iota(jnp.int32, (1, LANES), 1)
    even = ((lane >> 4) & 1) == 0
    nxt = pltpu.roll(x, LANES - 16, 1)
    prv = pltpu.roll(x, 16, 1)
    return jnp.where(even, nxt, prv)


def _rope(x, cos, sin):
    outs = []
    for c in range(x.shape[1] // LANES):
        xc = x[:, c * LANES:(c + 1) * LANES]
        outs.append(xc * cos + _swap16(xc) * sin)
    return outs[0] if len(outs) == 1 else jnp.concatenate(outs, axis=1)


def _prep_body(cur_ref, prv_ref, nxt_ref, sq_ref, skv_ref, ab_ref, cos_ref, sin_ref, cw_ref, gp_ref,
               qg_ref, kg_ref, bd_ref,
               oq, ok, ov, ogb, osq, osk, osv, *, tm, ctx, t):
    i = pl.program_id(1)
    rows = i * tm + lax.broadcasted_iota(jnp.int32, (tm, 1), 0)
    in_ctx = rows < ctx
    seg_lo = jnp.where(in_ctx, 0, ctx)
    seg_hi = jnp.where(in_ctx, ctx, t)
    xe = jnp.concatenate([prv_ref[0], cur_ref[0], nxt_ref[0]], axis=0)
    pad = DN_CONV // 2
    acc = None
    for j in range(DN_CONV):
        xs = xe[SUBLANES - pad + j:SUBLANES - pad + j + tm]
        if j != pad:
            n = rows + (j - pad)
            xs = jnp.where((n >= seg_lo) & (n < seg_hi), xs, 0.0)
        term = xs * cw_ref[j:j + 1, :]
        acc = term if acc is None else acc + term
    y = acc * _sigmoid(acc)
    bd = bd_ref[...]
    q = y[:, :DN_QK]
    k = y[:, DN_QK:2 * DN_QK]
    oq[0] = q * lax.rsqrt(_head_sum(q * q, bd) + NORM_EPS) * (DN_DK ** -0.5)
    ok[0] = k * lax.rsqrt(_head_sum(k * k, bd) + NORM_EPS)
    ov[0] = y[:, 2 * DN_QK:]

    ab = ab_ref[0]
    lane = lax.broadcasted_iota(jnp.int32, (1, LANES), 1)
    z = ab + gp_ref[1:2, :]
    softplus = jnp.maximum(z, 0.0) + jnp.log(1.0 + jnp.exp(-jnp.abs(z)))
    g = -jnp.exp(gp_ref[0:1, :]) * softplus
    ogb[0] = jnp.where(lane < 2 * DN_HEADS, g, _sigmoid(ab))

    cos = cos_ref[...]
    sin = sin_ref[...]
    sq = sq_ref[0]
    qn = sq * lax.rsqrt(_head_sum(sq * sq, bd) * (1.0 / SWA_HD) + NORM_EPS) * qg_ref[...]
    osq[0] = (_rope(qn, cos, sin) * (SWA_HD ** -0.5)).astype(BF16)
    skv = skv_ref[0]
    sk = skv[:, :D_KV]
    kn = sk * lax.rsqrt(_head_sum(sk * sk, bd[:D_KV, :D_KV]) * (1.0 / SWA_HD) + NORM_EPS) * kg_ref[...]
    osk[0] = _rope(kn, cos, sin).astype(BF16)
    osv[0] = skv[:, D_KV:].astype(BF16)


def _prep(pqkv, psq, pskv, pab, cos_t, sin_t, conv_w, gate_p, qg, kg, bd, ctx):
    n_batch, t, cq = pqkv.shape
    tm = TOK_TILE
    hb = tm // SUBLANES
    nh = t // SUBLANES
    body = functools.partial(_prep_body, tm=tm, ctx=ctx, t=t)
    tok = lambda w: pl.BlockSpec((1, tm, w), lambda b, i: (b, i, 0))
    full = lambda a: pl.BlockSpec(a.shape, lambda b, i: (0,) * a.ndim)
    return pl.pallas_call(
        body,
        grid=(n_batch, t // tm),
        in_specs=[tok(cq),
                  pl.BlockSpec((1, SUBLANES, cq), lambda b, i: (b, jnp.maximum(i * hb - 1, 0), 0)),
                  pl.BlockSpec((1, SUBLANES, cq), lambda b, i: (b, jnp.minimum((i + 1) * hb, nh - 1), 0)),
                  tok(D_SWA), tok(2 * D_KV), tok(LANES),
                  pl.BlockSpec((tm, LANES), lambda b, i: (i, 0)),
                  pl.BlockSpec((tm, LANES), lambda b, i: (i, 0)),
                  full(conv_w), full(gate_p), full(qg), full(kg), full(bd)],
        out_specs=[tok(DN_QK), tok(DN_QK), tok(D_DN), tok(LANES), tok(D_SWA), tok(D_KV), tok(D_KV)],
        out_shape=[jax.ShapeDtypeStruct((n_batch, t, DN_QK), F32),
                   jax.ShapeDtypeStruct((n_batch, t, DN_QK), F32),
                   jax.ShapeDtypeStruct((n_batch, t, D_DN), F32),
                   jax.ShapeDtypeStruct((n_batch, t, LANES), F32),
                   jax.ShapeDtypeStruct((n_batch, t, D_SWA), BF16),
                   jax.ShapeDtypeStruct((n_batch, t, D_KV), BF16),
                   jax.ShapeDtypeStruct((n_batch, t, D_KV), BF16)],
        compiler_params=_cparams(("parallel", "parallel")),
        name="prep",
    )(pqkv, pqkv, pqkv, psq, pskv, pab, cos_t, sin_t, conv_w, gate_p, qg, kg, bd)


def _tri_inverse(l_mat, eye):
    c = DN_CHUNK
    lb = l_mat.astype(BF16)
    p = jnp.dot(lb, lb, preferred_element_type=F32)
    x = eye - l_mat
    for _ in range(4):
        xp = jnp.concatenate([x, p], axis=0).astype(BF16)
        r = jnp.dot(xp, p.astype(BF16), preferred_element_type=F32)
        x = x + r[:c]
        p = r[c:]
    return x + jnp.dot(x.astype(BF16), p.astype(BF16), preferred_element_type=F32)


def _dn_chunk(q_ref, k_ref, v_ref, g_ref, o_ref, s_ref, tri_ref, row0, rev):
    c = DN_CHUNK
    r = pl.ds(pl.multiple_of(row0, c), c)
    q = q_ref[0, r, :]
    k = k_ref[0, r, :]
    v = v_ref[0, r, :]
    gb = g_ref[0, r, :]
    d = 1 if rev else 0
    tri = tri_ref[d]
    hi = gb.astype(BF16)
    r1 = gb - hi.astype(F32)
    mid = r1.astype(BF16)
    lo = (r1 - mid.astype(F32)).astype(BF16)
    gc = (jnp.dot(tri, hi, preferred_element_type=F32) + jnp.dot(tri, mid, preferred_element_type=F32)
          + jnp.dot(tri, lo, preferred_element_type=F32))
    last = 0 if rev else c - 1
    gl = gc[last:last + 1, :]
    eg = jnp.exp(gc)
    kdf = jnp.exp(gl - gc)
    egl = jnp.exp(gl)
    gct = gc.T
    ii = lax.broadcasted_iota(jnp.int32, (c, c), 0)
    jj = lax.broadcasted_iota(jnp.int32, (c, c), 1)
    incl = (ii <= jj) if rev else (ii >= jj)
    strict = (ii < jj) if rev else (ii > jj)
    eye = (ii == jj).astype(F32)
    outs = []
    for h in range(DN_HEADS):
        ln = d * DN_HEADS + h
        bl = 2 * DN_HEADS + ln
        qh = q[:, h * DN_DK:(h + 1) * DN_DK]
        kh = k[:, h * DN_DK:(h + 1) * DN_DK]
        vh = v[:, h * DN_DV:(h + 1) * DN_DV]
        beta = gb[:, bl:bl + 1]
        egc = eg[:, ln:ln + 1]
        diff = gc[:, ln:ln + 1] - gct[ln:ln + 1, :]
        decay = jnp.where(incl, jnp.exp(jnp.where(incl, diff, 0.0)), 0.0)
        kbh = kh * beta
        khb = kh.astype(BF16)
        lhs = jnp.concatenate([kbh, qh], axis=0).astype(BF16)
        kq = lax.dot_general(lhs, khb, (((1,), (1,)), ((), ())), preferred_element_type=F32)
        l_mat = jnp.where(strict, kq[:c] * decay, 0.0)
        aqk = kq[c:] * decay
        tinv = _tri_inverse(l_mat, eye).astype(BF16)
        u = jnp.dot(tinv, (vh * beta).astype(BF16), preferred_element_type=F32)
        w = jnp.dot(tinv, (kbh * egc).astype(BF16), preferred_element_type=F32)
        s = s_ref[d, h]
        lhs2 = jnp.concatenate([w, qh * egc], axis=0).astype(BF16)
        r2 = jnp.dot(lhs2, s.astype(BF16), preferred_element_type=F32)
        vnew = u - r2[:c]
        vnb = vnew.astype(BF16)
        outs.append(r2[c:] + jnp.dot(aqk.astype(BF16), vnb, preferred_element_type=F32))
        kdt = (kh * kdf[:, ln:ln + 1]).T.astype(BF16)
        s_ref[d, h] = s * egl[:, ln:ln + 1] + jnp.dot(kdt, vnb, preferred_element_type=F32)
    o_ref[0, r, :] = jnp.concatenate(outs, axis=1)


def _dn_body(qf, kf, vf, gf, qb, kb, vb, gbk, tri_ref, of_ref, ob_ref, s_ref, *, nchunk):
    i = pl.program_id(1)

    @pl.when(i == 0)
    def _():
        s_ref[...] = jnp.zeros_like(s_ref)

    def step(cidx, carry):
        _dn_chunk(qf, kf, vf, gf, of_ref, s_ref, tri_ref, cidx * DN_CHUNK, False)
        _dn_chunk(qb, kb, vb, gbk, ob_ref, s_ref, tri_ref, (nchunk - 1 - cidx) * DN_CHUNK, True)
        return carry

    lax.fori_loop(0, nchunk, step, 0)


def _deltanet(dq, dk, dv, gb, tri, ctx):
    n_batch, t, _ = dq.shape
    tb = TOK_TILE
    nblk = t // tb
    nctx = ctx // tb

    def bmap(i):
        return jnp.where(i < nctx, nctx - 1 - i, nblk - 1 - (i - nctx))

    fwd = lambda w: pl.BlockSpec((1, tb, w), lambda b, i: (b, i, 0))
    bwd = lambda w: pl.BlockSpec((1, tb, w), lambda b, i: (b, bmap(i), 0))
    body = functools.partial(_dn_body, nchunk=tb // DN_CHUNK)
    return pl.pallas_call(
        body,
        grid=(n_batch, nblk),
        in_specs=[fwd(DN_QK), fwd(DN_QK), fwd(D_DN), fwd(LANES),
                  bwd(DN_QK), bwd(DN_QK), bwd(D_DN), bwd(LANES),
                  pl.BlockSpec(tri.shape, lambda b, i: (0, 0, 0))],
        out_specs=[fwd(D_DN), bwd(D_DN)],
        out_shape=[jax.ShapeDtypeStruct((n_batch, t, D_DN), F32)] * 2,
        scratch_shapes=[pltpu.VMEM((2, DN_HEADS, DN_DK, DN_DV), F32)],
        compiler_params=_cparams(("arbitrary", "arbitrary")),
        name="deltanet",
    )(dq, dk, dv, gb, dq, dk, dv, gb, tri)


def _swa_body(sink_ref, q_ref, kp, kc, kn, vp, vc, vn, kx, vx, o_ref, *, ctx, t):
    i = pl.program_id(1)
    bl = SWA_BLOCK
    nband = 3 * bl
    ncol = nband + ctx
    q = q_ref[0]
    qpos = i * bl + lax.broadcasted_iota(jnp.int32, (bl, 1), 0)
    col = lax.broadcasted_iota(jnp.int32, (1, ncol), 1)
    kpos = (i - 1) * bl + col
    band_ok = ((jnp.abs(kpos - qpos) <= SWA_WINDOW) & (kpos >= ctx) & (kpos < t) & (qpos >= ctx)
               & (col < nband))
    mask = band_ok | (col >= nband)
    groups = SWA_Q_HEADS // SWA_KV_HEADS
    outs = []
    for kvh in range(SWA_KV_HEADS):
        sl = slice(kvh * SWA_HD, (kvh + 1) * SWA_HD)
        k_all = jnp.concatenate([kp[0][:, sl], kc[0][:, sl], kn[0][:, sl], kx[0][:, sl]], axis=0)
        v_all = jnp.concatenate([vp[0][:, sl], vc[0][:, sl], vn[0][:, sl], vx[0][:, sl]], axis=0)
        for g in range(groups):
            h = kvh * groups + g
            qh = q[:, h * SWA_HD:(h + 1) * SWA_HD]
            s = lax.dot_general(qh, k_all, (((1,), (1,)), ((), ())), preferred_element_type=F32)
            s = jnp.where(mask, s, NEG_INF)
            sink = sink_ref[h]
            m = jnp.maximum(jnp.max(s, axis=-1, keepdims=True), sink)
            p = jnp.exp(s - m)
            den = jnp.sum(p, axis=-1, keepdims=True) + jnp.exp(sink - m)
            o = jnp.dot(p.astype(BF16), v_all, preferred_element_type=F32)
            outs.append(o / den)
    o_ref[0] = jnp.concatenate(outs, axis=1).astype(BF16)


def _swa(sq, sk, sv, sinks, ctx):
    n_batch, t, _ = sq.shape
    bl = SWA_BLOCK
    nb = t // bl
    body = functools.partial(_swa_body, ctx=ctx, t=t)
    kvspec = lambda f: pl.BlockSpec((1, bl, D_KV), f)
    prev = lambda b, i: (b, jnp.maximum(i - 1, 0), 0)
    cur = lambda b, i: (b, i, 0)
    nxt = lambda b, i: (b, jnp.minimum(i + 1, nb - 1), 0)
    cspec = pl.BlockSpec((1, ctx, D_KV), lambda b, i: (b, 0, 0))
    return pl.pallas_call(
        body,
        grid=(n_batch, nb),
        in_specs=[pl.BlockSpec(memory_space=pltpu.SMEM),
                  pl.BlockSpec((1, bl, D_SWA), cur),
                  kvspec(prev), kvspec(cur), kvspec(nxt),
                  kvspec(prev), kvspec(cur), kvspec(nxt),
                  cspec, cspec],
        out_specs=pl.BlockSpec((1, bl, D_SWA), cur),
        out_shape=jax.ShapeDtypeStruct((n_batch, t, D_SWA), BF16),
        compiler_params=_cparams(("parallel", "parallel")),
        name="swa",
    )(sinks, sq, sk, sk, sk, sv, sv, sv, sk, sv)


def _post_body(of_ref, ob_ref, z_ref, sw_ref, x_ref, mod_ref, og_ref, wo_ref, g2_ref, rw_ref, rb_ref,
               bd_ref, tri_ref, xo_ref, h_ref, route_ref, cnt_ref, *, tm, ctx, n_batch, d):
    b = pl.program_id(0)
    i = pl.program_id(1)
    rows = i * tm + lax.broadcasted_iota(jnp.int32, (tm, 1), 0)
    is_ctx = rows < ctx
    o = of_ref[0] + ob_ref[0]
    ms = _head_sum(o * o, bd_ref[...]) * (1.0 / DN_DV)
    z = z_ref[0]
    dn = o * lax.rsqrt(ms + NORM_EPS) * og_ref[...] * (z * _sigmoid(z))
    mix = jnp.concatenate([dn.astype(BF16), sw_ref[0]], axis=1)
    proj = jnp.dot(mix, wo_ref[...], preferred_element_type=F32)
    gt1 = _mod_chunk(mod_ref, b, n_batch, is_ctx, 2, d)
    x = x_ref[0] + gt1 * proj
    xo_ref[0] = x

    ms2 = jnp.mean(x * x, axis=-1, keepdims=True)
    y = x * lax.rsqrt(ms2 + NORM_EPS) * g2_ref[...]
    sh2 = _mod_chunk(mod_ref, b, n_batch, is_ctx, 3, d)
    sc2 = _mod_chunk(mod_ref, b, n_batch, is_ctx, 4, d)
    h = y * (1.0 + sc2) + sh2
    h_ref[0] = h

    logits = jnp.dot(h, rw_ref[...], precision=HIGHEST, preferred_element_type=F32) + rb_ref[...]
    lane = lax.broadcasted_iota(jnp.int32, (1, LANES), 1)
    vals, idxs = [], []
    l = logits
    for _ in range(TOP_K):
        m = jnp.max(l, axis=-1, keepdims=True)
        idx = jnp.min(jnp.where(l == m, lane, LANES), axis=-1, keepdims=True)
        vals.append(m)
        idxs.append(idx)
        l = jnp.where(lane == idx, -3e38, l)
    es = [jnp.exp(v - vals[0]) for v in vals]
    den = es[0] + es[1] + es[2] + es[3]
    gates = [e / den for e in es]

    @pl.when((b == 0) & (i == 0))
    def _():
        cnt_ref[...] = jnp.zeros_like(cnt_ref)

    onehot = jnp.zeros((tm, LANES), F32)
    for idx in idxs:
        onehot = onehot + (lane == idx).astype(F32)
    run = cnt_ref[0:1, :]
    cum = jnp.dot(tri_ref[...], onehot.astype(BF16), preferred_element_type=F32) + run
    ranks = [jnp.sum(jnp.where(lane == idx, cum, 0.0), axis=-1, keepdims=True) for idx in idxs]
    cnt_ref[...] = jnp.broadcast_to(run + jnp.sum(onehot, axis=0, keepdims=True), cnt_ref.shape)
    route = jnp.zeros((tm, LANES), F32)
    for k in range(TOP_K):
        route = jnp.where(lane == k, idxs[k].astype(F32), route)
        route = jnp.where(lane == TOP_K + k, gates[k], route)
        route = jnp.where(lane == 2 * TOP_K + k, ranks[k], route)
    route_ref[0] = route


def _post(o_f, o_b, pz, o_sw, xu, mod, og, w_out, g2, rw, rb, bd, tri, ctx):
    n_batch, t, d = xu.shape
    tm = TOK_TILE
    body = functools.partial(_post_body, tm=tm, ctx=ctx, n_batch=n_batch, d=d)
    tok = lambda w: pl.BlockSpec((1, tm, w), lambda b, i: (b, i, 0))
    full = lambda a: pl.BlockSpec(a.shape, lambda b, i: (0,) * a.ndim)
    return pl.pallas_call(
        body,
        grid=(n_batch, t // tm),
        in_specs=[tok(D_DN), tok(D_DN), tok(D_DN), tok(D_SWA), tok(d),
                  full(mod), full(og), full(w_out), full(g2), full(rw), full(rb), full(bd), full(tri)],
        out_specs=[tok(d), tok(d), tok(LANES), pl.BlockSpec((SUBLANES, LANES), lambda b, i: (0, 0))],
        out_shape=[jax.ShapeDtypeStruct((n_batch, t, d), F32),
                   jax.ShapeDtypeStruct((n_batch, t, d), F32),
                   jax.ShapeDtypeStruct((n_batch, t, LANES), F32),
                   jax.ShapeDtypeStruct((SUBLANES, LANES), F32)],
        compiler_params=_cparams(("arbitrary", "arbitrary")),
        name="post_attn",
    )(o_f, o_b, pz, o_sw, xu, mod, og, w_out, g2, rw, rb, bd, tri)


def _dispatch_body(dest_ref, h_ref, xs_ref, sem, *, td):
    i = pl.program_id(0)

    def issue(tk, carry):
        for k in range(TOP_K):
            dst = dest_ref[0, 0, tk * TOP_K + k]
            pltpu.make_async_copy(h_ref.at[pl.ds(i * td + tk, 1)], xs_ref.at[pl.ds(dst, 1)], sem).start()
        return carry

    lax.fori_loop(0, td, issue, 0)

    def drain(tk, carry):
        for k in range(TOP_K):
            pltpu.make_async_copy(h_ref.at[pl.ds(0, 1)], xs_ref.at[pl.ds(0, 1)], sem).wait()
        return carry

    lax.fori_loop(0, td, drain, 0)


def _dispatch(h_flat, dest, n_slots):
    ntok, d = h_flat.shape
    td = TOK_TILE
    body = functools.partial(_dispatch_body, td=td)
    return pl.pallas_call(
        body,
        grid=(ntok // td,),
        in_specs=[pl.BlockSpec((1, 1, td * TOP_K), lambda i: (i, 0, 0), memory_space=pltpu.SMEM),
                  pl.BlockSpec(memory_space=pl.ANY)],
        out_specs=pl.BlockSpec(memory_space=pl.ANY),
        out_shape=jax.ShapeDtypeStruct((n_slots, d), F32),
        scratch_shapes=[pltpu.SemaphoreType.DMA(())],
        compiler_params=_cparams(("arbitrary",)),
        name="moe_dispatch",
    )(dest.reshape(ntok // td, 1, td * TOP_K), h_flat)


def _moe_body(te_ref, nv_ref, nu_ref, xs_ref, wgu_ref, bgu_ref, wdn_ref, bdn_ref, ys_ref, *, tm, de):
    j = pl.program_id(0)

    @pl.when(j < nu_ref[0])
    def _():
        rows = lax.broadcasted_iota(jnp.int32, (tm, 1), 0)
        x = jnp.where(rows < nv_ref[j], xs_ref[...], 0.0).astype(BF16)
        gu = jnp.dot(x, wgu_ref[0], preferred_element_type=F32) + bgu_ref[0]
        gate = jnp.minimum(gu[:, :de], SWIGLU_LIMIT)
        up = jnp.clip(gu[:, de:], -SWIGLU_LIMIT, SWIGLU_LIMIT)
        act = (up + 1.0) * gate * _sigmoid(SWIGLU_ALPHA * gate)
        ys_ref[...] = jnp.dot(act.astype(BF16), wdn_ref[0], preferred_element_type=F32) + bdn_ref[0]


def _moe(xs, tile_expert, tile_valid, n_used, wgu, bgu, wdn, bdn):
    n_slots, d = xs.shape
    tm = MOE_TILE
    n_exp, _, de2 = wgu.shape
    de = de2 // 2
    body = functools.partial(_moe_body, tm=tm, de=de)
    grid_spec = pltpu.PrefetchScalarGridSpec(
        num_scalar_prefetch=3,
        grid=(n_slots // tm,),
        in_specs=[pl.BlockSpec((tm, d), lambda j, te, nv, nu: (j, 0)),
                  pl.BlockSpec((1, d, de2), lambda j, te, nv, nu: (te[j], 0, 0)),
                  pl.BlockSpec((1, 1, de2), lambda j, te, nv, nu: (te[j], 0, 0)),
                  pl.BlockSpec((1, de, d), lambda j, te, nv, nu: (te[j], 0, 0)),
                  pl.BlockSpec((1, 1, d), lambda j, te, nv, nu: (te[j], 0, 0))],
        out_specs=pl.BlockSpec((tm, d), lambda j, te, nv, nu: (j, 0)),
    )
    return pl.pallas_call(
        body,
        grid_spec=grid_spec,
        out_shape=jax.ShapeDtypeStruct((n_slots, d), F32),
        compiler_params=_cparams(("arbitrary",)),
        name="moe_experts",
    )(tile_expert, tile_valid, n_used, xs, wgu, bgu.reshape(n_exp, 1, de2), wdn, bdn.reshape(n_exp, 1, d))


def _combine_body(dest_ref, ys_ref, x_ref, route_ref, mod_ref, o_ref, buf, sem, *, tc, ctx, n_batch, d):
    b = pl.program_id(0)
    i = pl.program_id(1)

    def issue(tk, carry):
        for k in range(TOP_K):
            src = dest_ref[0, 0, tk * TOP_K + k]
            pltpu.make_async_copy(ys_ref.at[pl.ds(src, 1)], buf.at[k, pl.ds(tk, 1)], sem).start()
        return carry

    lax.fori_loop(0, tc, issue, 0)

    def drain(tk, carry):
        for k in range(TOP_K):
            pltpu.make_async_copy(ys_ref.at[pl.ds(0, 1)], buf.at[k, pl.ds(0, 1)], sem).wait()
        return carry

    lax.fori_loop(0, tc, drain, 0)

    route = route_ref[0]
    y = None
    for k in range(TOP_K):
        term = route[:, TOP_K + k:TOP_K + k + 1] * buf[k]
        y = term if y is None else y + term
    rows = i * tc + lax.broadcasted_iota(jnp.int32, (tc, 1), 0)
    gt2 = _mod_chunk(mod_ref, b, n_batch, rows < ctx, 5, d)
    o_ref[0] = x_ref[0] + gt2 * y


def _combine(dest, ys, xu, route, mod, ctx):
    n_batch, t, d = xu.shape
    tc = TOK_TILE
    nt = t // tc
    body = functools.partial(_combine_body, tc=tc, ctx=ctx, n_batch=n_batch, d=d)
    return pl.pallas_call(
        body,
        grid=(n_batch, nt),
        in_specs=[pl.BlockSpec((1, 1, tc * TOP_K), lambda b, i: (b * nt + i, 0, 0), memory_space=pltpu.SMEM),
                  pl.BlockSpec(memory_space=pl.ANY),
                  pl.BlockSpec((1, tc, d), lambda b, i: (b, i, 0)),
                  pl.BlockSpec((1, tc, LANES), lambda b, i: (b, i, 0)),
                  pl.BlockSpec(mod.shape, lambda b, i: (0, 0))],
        out_specs=pl.BlockSpec((1, tc, d), lambda b, i: (b, i, 0)),
        out_shape=jax.ShapeDtypeStruct((n_batch, t, d), F32),
        scratch_shapes=[pltpu.VMEM((TOP_K, tc, d), F32), pltpu.SemaphoreType.DMA(())],
        compiler_params=_cparams(("arbitrary", "arbitrary")),
        name="moe_combine",
    )(dest.reshape(n_batch * nt, 1, tc * TOP_K), ys, xu, route, mod)


def _rope_tables(seq, ctx):
    rows = seq // GRID_W
    row = jnp.repeat(jnp.arange(rows, dtype=F32), GRID_W)
    col = jnp.tile(jnp.arange(GRID_W, dtype=F32), rows)
    inv_freq = jnp.power(ROPE_THETA, -jnp.arange(ROPE_FREQS, dtype=F32) / ROPE_FREQS)
    ang_r = row[:, None] * inv_freq
    ang_c = col[:, None] * inv_freq
    cos = jnp.concatenate([jnp.cos(ang_r), jnp.cos(ang_r), jnp.cos(ang_c), jnp.cos(ang_c)], axis=1)
    sin = jnp.concatenate([-jnp.sin(ang_r), jnp.sin(ang_r), -jnp.sin(ang_c), jnp.sin(ang_c)], axis=1)
    cos = jnp.concatenate([jnp.ones((ctx, SWA_HD), F32), cos], axis=0)
    sin = jnp.concatenate([jnp.zeros((ctx, SWA_HD), F32), sin], axis=0)
    reps = LANES // SWA_HD
    return jnp.tile(cos, (1, reps)), jnp.tile(sin, (1, reps))


def _pad_w_in(w_in):
    sizes = (DN_QK, DN_QK, D_DN, D_DN, 2 * DN_HEADS, 2 * DN_HEADS, D_SWA, D_KV, D_KV)
    offs = np.concatenate([[0], np.cumsum(sizes)])
    part = lambda n: w_in[:, offs[n]:offs[n + 1]]
    pad = jnp.zeros((w_in.shape[0], LANES - 4 * DN_HEADS), w_in.dtype)
    return jnp.concatenate([part(0), part(1), part(2), part(3), part(6), part(7), part(8), part(4), part(5), pad],
                           axis=1).astype(BF16)


def _lane_row(v, fill=0.0):
    out = jnp.full((1, LANES), fill, F32)
    return out.at[0, :v.shape[0]].set(v.astype(F32))


def kernel(x, c, ctx, c_ctx, ada_w, ada_b, norm1_g, w_in, dn_conv_w, dn_a_log, dn_dt_bias, dn_out_g, q_norm_g,
           k_norm_g, sinks, w_out, norm2_g, router_w, router_b, w_gate_up, b_gate_up, w_down, b_down):
    n_batch, seq, d = x.shape
    n_ctx = ctx.shape[1]
    depth = ada_w.shape[0]
    t = n_ctx + seq
    ntok = n_batch * t
    assert n_batch < SUBLANES and n_ctx % TOK_TILE == 0 and seq % TOK_TILE == 0

    xu = jnp.concatenate([ctx, x], axis=1)
    cvec = jnp.zeros((SUBLANES, d), F32).at[:n_batch].set(c).at[n_batch].set(c_ctx)
    cos_t, sin_t = _rope_tables(seq, n_ctx)
    hid = np.arange(DN_QK) // DN_DK
    bd = jnp.asarray(hid[:, None] == hid[None, :], BF16)
    cidx = np.arange(DN_CHUNK)
    tri_dn = jnp.asarray(np.stack([cidx[:, None] >= cidx[None, :], cidx[:, None] <= cidx[None, :]]), BF16)
    ridx = np.arange(TOK_TILE)
    tri_rank = jnp.asarray(ridx[:, None] > ridx[None, :], BF16)

    n_slots = ntok * TOP_K + N_EXPERTS * MOE_TILE
    n_tiles = n_slots // MOE_TILE
    eid = jnp.arange(N_EXPERTS, dtype=jnp.int32)

    for l in range(depth):
        mod = _ada(cvec, ada_w[l], ada_b[l])
        pqkv, pz, psq, pskv, pab = _inproj(xu, mod, norm1_g[l], _pad_w_in(w_in[l]), n_ctx)
        conv_w = jnp.zeros((SUBLANES, dn_conv_w.shape[2]), F32).at[:DN_CONV].set(dn_conv_w[l])
        gate_p = jnp.concatenate([_lane_row(dn_a_log[l].reshape(-1)), _lane_row(dn_dt_bias[l].reshape(-1)),
                                  jnp.zeros((SUBLANES - 2, LANES), F32)], axis=0)
        qg = jnp.tile(q_norm_g[l], SWA_Q_HEADS).reshape(1, D_SWA)
        kg = jnp.tile(k_norm_g[l], SWA_KV_HEADS).reshape(1, D_KV)
        dq, dk, dv, gb, sq, sk, sv = _prep(pqkv, psq, pskv, pab, cos_t, sin_t, conv_w, gate_p, qg, kg, bd, n_ctx)
        o_f, o_b = _deltanet(dq, dk, dv, gb, tri_dn, n_ctx)
        o_sw = _swa(sq, sk, sv, sinks[l], n_ctx)
        og = jnp.tile(dn_out_g[l], DN_HEADS).reshape(1, D_DN)
        rw = jnp.zeros((d, LANES), F32).at[:, :N_EXPERTS].set(router_w[l])
        rb = _lane_row(router_b[l], NEG_INF)
        xu, h, route, cnt = _post(o_f, o_b, pz, o_sw, xu, mod, og, w_out[l].astype(BF16),
                                  norm2_g[l].reshape(1, d), rw, rb, bd, tri_rank, n_ctx)

        route_flat = route.reshape(ntok, LANES)
        idx = route_flat[:, :TOP_K].astype(jnp.int32)
        rank = route_flat[:, 2 * TOP_K:3 * TOP_K].astype(jnp.int32)
        counts = cnt[0, :N_EXPERTS].astype(jnp.int32)
        padded = (counts + MOE_TILE - 1) // MOE_TILE * MOE_TILE
        pad_end = jnp.cumsum(padded)
        pad_start = pad_end - padded
        dest = jnp.sum(jnp.where(idx[:, :, None] == eid[None, None, :], pad_start[None, None, :], 0), axis=-1) + rank
        tile_lo = jnp.arange(n_tiles, dtype=jnp.int32) * MOE_TILE
        tile_expert = jnp.minimum(jnp.sum(pad_end[None, :] <= tile_lo[:, None], axis=1), N_EXPERTS - 1).astype(jnp.int32)
        tile_valid = jnp.clip((pad_start + counts)[tile_expert] - tile_lo, 0, MOE_TILE).astype(jnp.int32)
        n_used = (pad_end[-1:] // MOE_TILE).astype(jnp.int32)

        xs = _dispatch(h.reshape(ntok, d), dest, n_slots)
        ys = _moe(xs, tile_expert, tile_valid, n_used, w_gate_up[l].astype(BF16), b_gate_up[l],
                  w_down[l].astype(BF16), b_down[l])
        xu = _combine(dest, ys, xu, route, mod, n_ctx)

    return xu[:, n_ctx:, :]
```

```python
import functools
import math

import jax
import jax.numpy as jnp
import numpy as np
from jax import lax
from jax.experimental import pallas as pl
from jax.experimental.pallas import tpu as pltpu

F32 = jnp.float32
BF16 = jnp.bfloat16
HIGHEST = lax.Precision.HIGHEST

GRID_W = 64
DN_HEADS = 8
DN_DK = 64
DN_DV = 64
DN_CONV = 5
DN_CHUNK = 64
SWA_Q_HEADS = 8
SWA_KV_HEADS = 2
SWA_HD = 64
SWA_WINDOW = 128
SWA_BLOCK = 128
ROPE_THETA = 10000.0
ROPE_FREQS = SWA_HD // 4
DN_QK = DN_HEADS * DN_DK
D_DN = DN_HEADS * DN_DV
D_SWA = SWA_Q_HEADS * SWA_HD
D_KV = SWA_KV_HEADS * SWA_HD
N_EXPERTS = 32
TOP_K = 4
SWIGLU_LIMIT = 7.0
SWIGLU_ALPHA = 1.702
NORM_EPS = 1e-6
NEG_INF = -1e30

LANES = 128
SUBLANES = 8
TOK_TILE = 256
MOE_TILE = 256
VMEM_LIMIT = 48 * 1024 * 1024


def _sigmoid(x):
    return 1.0 / (1.0 + jnp.exp(-x))


def _cparams(sem, vmem=None):
    return pltpu.CompilerParams(dimension_semantics=sem, vmem_limit_bytes=vmem or VMEM_LIMIT)


def _head_sum(x2, bd):
    hi = x2.astype(BF16)
    lo = (x2 - hi.astype(F32)).astype(BF16)
    return (jnp.dot(hi, bd, preferred_element_type=F32) + jnp.dot(lo, bd, preferred_element_type=F32))


def _mod_chunk(mod_ref, b, n_batch, is_ctx, k, d):
    mb = mod_ref[pl.ds(b, 1), k * d:(k + 1) * d]
    mc = mod_ref[n_batch:n_batch + 1, k * d:(k + 1) * d]
    return jnp.where(is_ctx, mc, mb)


def _ada_body(c_ref, w_ref, b_ref, o_ref):
    c = c_ref[...]
    s = c * _sigmoid(c)
    o_ref[...] = jnp.dot(s, w_ref[...], precision=HIGHEST, preferred_element_type=F32) + b_ref[...]


def _ada(cvec, ada_w, ada_b):
    d, n = ada_w.shape
    tn = n // 4
    return pl.pallas_call(
        _ada_body,
        grid=(n // tn,),
        in_specs=[pl.BlockSpec((SUBLANES, d), lambda j: (0, 0)),
                  pl.BlockSpec((d, tn), lambda j: (0, j)),
                  pl.BlockSpec((1, tn), lambda j: (0, j))],
        out_specs=pl.BlockSpec((SUBLANES, tn), lambda j: (0, j)),
        out_shape=jax.ShapeDtypeStruct((SUBLANES, n), F32),
        compiler_params=_cparams(("arbitrary",)),
        name="ada_mod",
    )(cvec, ada_w, ada_b.reshape(1, n))


def _inproj_body(x_ref, mod_ref, g_ref, w_ref, oqkv, oz, osq, oskv, oab, *, tm, ctx, n_batch, d):
    b = pl.program_id(0)
    i = pl.program_id(1)
    x = x_ref[0]
    ms = jnp.mean(x * x, axis=-1, keepdims=True)
    y = x * lax.rsqrt(ms + NORM_EPS) * g_ref[...]
    rows = i * tm + lax.broadcasted_iota(jnp.int32, (tm, 1), 0)
    is_ctx = rows < ctx
    sh = _mod_chunk(mod_ref, b, n_batch, is_ctx, 0, d)
    sc = _mod_chunk(mod_ref, b, n_batch, is_ctx, 1, d)
    h = (y * (1.0 + sc) + sh).astype(BF16)
    p = jnp.dot(h, w_ref[...], preferred_element_type=F32)
    c0 = 2 * DN_QK + D_DN
    oqkv[0] = p[:, :c0]
    oz[0] = p[:, c0:c0 + D_DN]
    c1 = c0 + D_DN
    osq[0] = p[:, c1:c1 + D_SWA]
    c2 = c1 + D_SWA
    oskv[0] = p[:, c2:c2 + 2 * D_KV]
    c3 = c2 + 2 * D_KV
    oab[0] = p[:, c3:c3 + LANES]


def _inproj(xu, mod, g1, w_pad, ctx):
    n_batch, t, d = xu.shape
    tm = TOK_TILE
    npad = w_pad.shape[1]
    widths = (2 * DN_QK + D_DN, D_DN, D_SWA, 2 * D_KV, LANES)
    body = functools.partial(_inproj_body, tm=tm, ctx=ctx, n_batch=n_batch, d=d)
    return pl.pallas_call(
        body,
        grid=(n_batch, t // tm),
        in_specs=[pl.BlockSpec((1, tm, d), lambda b, i: (b, i, 0)),
                  pl.BlockSpec((SUBLANES, 6 * d), lambda b, i: (0, 0)),
                  pl.BlockSpec((1, d), lambda b, i: (0, 0)),
                  pl.BlockSpec((d, npad), lambda b, i: (0, 0))],
        out_specs=[pl.BlockSpec((1, tm, w), lambda b, i: (b, i, 0)) for w in widths],
        out_shape=[jax.ShapeDtypeStruct((n_batch, t, w), F32) for w in widths],
        compiler_params=_cparams(("parallel", "parallel")),
        name="inproj",
    )(xu, mod, g1.reshape(1, d), w_pad)


def _swap16(x):
    lane = lax.broadcasted_iota(jnp.int32, (1, LANES), 1)
    even = ((lane >> 4) & 1) == 0
    nxt = pltpu.roll(x, LANES - 16, 1)
    prv = pltpu.roll(x, 16, 1)
    return jnp.where(even, nxt, prv)


def _rope(x, cos, sin):
    outs = []
    for c in range(x.shape[1] // LANES):
        xc = x[:, c * LANES:(c + 1) * LANES]
        outs.append(xc * cos + _swap16(xc) * sin)
    return outs[0] if len(outs) == 1 else jnp.concatenate(outs, axis=1)


def _prep_body(cur_ref, prv_ref, nxt_ref, sq_ref, skv_ref, ab_ref, cos_ref, sin_ref, cw_ref, gp_ref,
               qg_ref, kg_ref, bd_ref,
               oq, ok, ov, ogb, osq, osk, osv, *, tm, ctx, t):
    i = pl.program_id(1)
    rows = i * tm + lax.broadcasted_iota(jnp.int32, (tm, 1), 0)
    in_ctx = rows < ctx
    seg_lo = jnp.where(in_ctx, 0, ctx)
    seg_hi = jnp.where(in_ctx, ctx, t)
    xe = jnp.concatenate([prv_ref[0], cur_ref[0], nxt_ref[0]], axis=0)
    pad = DN_CONV // 2
    acc = None
    for j in range(DN_CONV):
        xs = xe[SUBLANES - pad + j:SUBLANES - pad + j + tm]
        if j != pad:
            n = rows + (j - pad)
            xs = jnp.where((n >= seg_lo) & (n < seg_hi), xs, 0.0)
        term = xs * cw_ref[j:j + 1, :]
        acc = term if acc is None else acc + term
    y = acc * _sigmoid(acc)
    bd = bd_ref[...]
    q = y[:, :DN_QK]
    k = y[:, DN_QK:2 * DN_QK]
    oq[0] = q * lax.rsqrt(_head_sum(q * q, bd) + NORM_EPS) * (DN_DK ** -0.5)
    ok[0] = k * lax.rsqrt(_head_sum(k * k, bd) + NORM_EPS)
    ov[0] = y[:, 2 * DN_QK:]

    ab = ab_ref[0]
    lane = lax.broadcasted_iota(jnp.int32, (1, LANES), 1)
    z = ab + gp_ref[1:2, :]
    softplus = jnp.maximum(z, 0.0) + jnp.log(1.0 + jnp.exp(-jnp.abs(z)))
    g = -jnp.exp(gp_ref[0:1, :]) * softplus
    ogb[0] = jnp.where(lane < 2 * DN_HEADS, g, _sigmoid(ab))

    cos = cos_ref[...]
    sin = sin_ref[...]
    sq = sq_ref[0]
    qn = sq * lax.rsqrt(_head_sum(sq * sq, bd) * (1.0 / SWA_HD) + NORM_EPS) * qg_ref[...]
    osq[0] = (_rope(qn, cos, sin) * (SWA_HD ** -0.5)).astype(BF16)
    skv = skv_ref[0]
    sk = skv[:, :D_KV]
    kn = sk * lax.rsqrt(_head_sum(sk * sk, bd[:D_KV, :D_KV]) * (1.0 / SWA_HD) + NORM_EPS) * kg_ref[...]
    osk[0] = _rope(kn, cos, sin).astype(BF16)
    osv[0] = skv[:, D_KV:].astype(BF16)


def _prep(pqkv, psq, pskv, pab, cos_t, sin_t, conv_w, gate_p, qg, kg, bd, ctx):
    n_batch, t, cq = pqkv.shape
    tm = TOK_TILE
    hb = tm // SUBLANES
    nh = t // SUBLANES
    body = functools.partial(_prep_body, tm=tm, ctx=ctx, t=t)
    tok = lambda w: pl.BlockSpec((1, tm, w), lambda b, i: (b, i, 0))
    full = lambda a: pl.BlockSpec(a.shape, lambda b, i: (0,) * a.ndim)
    return pl.pallas_call(
        body,
        grid=(n_batch, t // tm),
        in_specs=[tok(cq),
                  pl.BlockSpec((1, SUBLANES, cq), lambda b, i: (b, jnp.maximum(i * hb - 1, 0), 0)),
                  pl.BlockSpec((1, SUBLANES, cq), lambda b, i: (b, jnp.minimum((i + 1) * hb, nh - 1), 0)),
                  tok(D_SWA), tok(2 * D_KV), tok(LANES),
                  pl.BlockSpec((tm, LANES), lambda b, i: (i, 0)),
                  pl.BlockSpec((tm, LANES), lambda b, i: (i, 0)),
                  full(conv_w), full(gate_p), full(qg), full(kg), full(bd)],
        out_specs=[tok(DN_QK), tok(DN_QK), tok(D_DN), tok(LANES), tok(D_SWA), tok(D_KV), tok(D_KV)],
        out_shape=[jax.ShapeDtypeStruct((n_batch, t, DN_QK), F32),
                   jax.ShapeDtypeStruct((n_batch, t, DN_QK), F32),
                   jax.ShapeDtypeStruct((n_batch, t, D_DN), F32),
                   jax.ShapeDtypeStruct((n_batch, t, LANES), F32),
                   jax.ShapeDtypeStruct((n_batch, t, D_SWA), BF16),
                   jax.ShapeDtypeStruct((n_batch, t, D_KV), BF16),
                   jax.ShapeDtypeStruct((n_batch, t, D_KV), BF16)],
        compiler_params=_cparams(("parallel", "parallel")),
        name="prep",
    )(pqkv, pqkv, pqkv, psq, pskv, pab, cos_t, sin_t, conv_w, gate_p, qg, kg, bd)


def _dn_dir_setup(q_ref, k_ref, v_ref, g_ref, tri, row0, rev):
    c = DN_CHUNK
    r = pl.ds(pl.multiple_of(row0, c), c)
    gb = g_ref[0, r, :]
    hi = gb.astype(BF16)
    r1 = gb - hi.astype(F32)
    mid = r1.astype(BF16)
    lo = (r1 - mid.astype(F32)).astype(BF16)
    gc = (jnp.dot(tri, hi, preferred_element_type=F32) + jnp.dot(tri, mid, preferred_element_type=F32)
          + jnp.dot(tri, lo, preferred_element_type=F32))
    last = 0 if rev else c - 1
    gl = gc[last:last + 1, :]
    ii = lax.broadcasted_iota(jnp.int32, (c, c), 0)
    jj = lax.broadcasted_iota(jnp.int32, (c, c), 1)
    return dict(r=r, q=q_ref[0, r, :], k=k_ref[0, r, :], v=v_ref[0, r, :], gb=gb, gc=gc, gct=gc.T,
                eg=jnp.exp(gc), kdf=jnp.exp(gl - gc), egl=jnp.exp(gl),
                incl=(ii <= jj) if rev else (ii >= jj), strict=(ii < jj) if rev else (ii > jj),
                eye=(ii == jj).astype(F32))


def _dn_pair(fwd_refs, bwd_refs, of_ref, ob_ref, s_ref, tri_ref, row_f, row_b):
    c = DN_CHUNK
    dirs = [_dn_dir_setup(*fwd_refs, tri_ref[0], row_f, False), _dn_dir_setup(*bwd_refs, tri_ref[1], row_b, True)]
    chains = [(d, h) for d in range(2) for h in range(DN_HEADS)]
    dot = functools.partial(jnp.dot, preferred_element_type=F32)

    def head(name, d, h, width):
        return dirs[d][name][:, h * width:(h + 1) * width]

    def col(name, d, h, off=0):
        ln = off + d * DN_HEADS + h
        return dirs[d][name][:, ln:ln + 1]

    kh = [head("k", d, h, DN_DK) for d, h in chains]
    qh = [head("q", d, h, DN_DK) for d, h in chains]
    vh = [head("v", d, h, DN_DV) for d, h in chains]
    beta = [col("gb", d, h, 2 * DN_HEADS) for d, h in chains]
    egc = [col("eg", d, h) for d, h in chains]
    decay = []
    for d, h in chains:
        ln = d * DN_HEADS + h
        incl = dirs[d]["incl"]
        diff = dirs[d]["gc"][:, ln:ln + 1] - dirs[d]["gct"][ln:ln + 1, :]
        decay.append(jnp.where(incl, jnp.exp(jnp.where(incl, diff, 0.0)), 0.0))
    kbh = [k * b for k, b in zip(kh, beta)]
    kq = [lax.dot_general(jnp.concatenate([kb, q], axis=0).astype(BF16), k.astype(BF16),
                          (((1,), (1,)), ((), ())), preferred_element_type=F32)
          for kb, q, k in zip(kbh, qh, kh)]
    lmat = [jnp.where(dirs[d]["strict"], a[:c] * dc, 0.0) for (d, h), a, dc in zip(chains, kq, decay)]
    aqk = [a[c:] * dc for a, dc in zip(kq, decay)]
    lb = [m.astype(BF16) for m in lmat]
    p = [dot(m, m) for m in lb]
    x = [dirs[d]["eye"] - m for (d, h), m in zip(chains, lmat)]
    for _ in range(4):
        r = [dot(jnp.concatenate([xi, pi], axis=0).astype(BF16), pi.astype(BF16)) for xi, pi in zip(x, p)]
        x = [xi + ri[:c] for xi, ri in zip(x, r)]
        p = [ri[c:] for ri in r]
    tinv = [(xi + dot(xi.astype(BF16), pi.astype(BF16))).astype(BF16) for xi, pi in zip(x, p)]
    uw = [dot(ti, jnp.concatenate([v * b, kb * e], axis=1).astype(BF16))
          for ti, v, b, kb, e in zip(tinv, vh, beta, kbh, egc)]
    kdt = [(k * col("kdf", d, h)).T for (d, h), k in zip(chains, kh)]
    s = [s_ref[d, h] for d, h in chains]
    r2 = [dot(jnp.concatenate([a[:, DN_DV:], q * e], axis=0).astype(BF16), si.astype(BF16))
          for a, q, e, si in zip(uw, qh, egc, s)]
    vnb = [(a[:, :DN_DV] - ri[:c]).astype(BF16) for a, ri in zip(uw, r2)]
    r3 = [dot(jnp.concatenate([a, kt], axis=0).astype(BF16), vn) for a, kt, vn in zip(aqk, kdt, vnb)]
    for n, (d, h) in enumerate(chains):
        s_ref[d, h] = s[n] * col("egl", d, h) + r3[n][c:]
    outs = [ri[c:] + r3i[:c] for ri, r3i in zip(r2, r3)]
    of_ref[0, dirs[0]["r"], :] = jnp.concatenate(outs[:DN_HEADS], axis=1)
    ob_ref[0, dirs[1]["r"], :] = jnp.concatenate(outs[DN_HEADS:], axis=1)


def _dn_body(qf, kf, vf, gf, qb, kb, vb, gbk, tri_ref, of_ref, ob_ref, s_ref, *, nchunk):
    i = pl.program_id(1)

    @pl.when(i == 0)
    def _():
        s_ref[...] = jnp.zeros_like(s_ref)

    def step(cidx, carry):
        _dn_pair((qf, kf, vf, gf), (qb, kb, vb, gbk), of_ref, ob_ref, s_ref, tri_ref,
                 cidx * DN_CHUNK, (nchunk - 1 - cidx) * DN_CHUNK)
        return carry

    lax.fori_loop(0, nchunk, step, 0)


def _deltanet(dq, dk, dv, gb, tri, ctx):
    n_batch, t, _ = dq.shape
    tb = TOK_TILE
    nblk = t // tb
    nctx = ctx // tb

    def bmap(i):
        return jnp.where(i < nctx, nctx - 1 - i, nblk - 1 - (i - nctx))

    fwd = lambda w: pl.BlockSpec((1, tb, w), lambda b, i: (b, i, 0))
    bwd = lambda w: pl.BlockSpec((1, tb, w), lambda b, i: (b, bmap(i), 0))
    body = functools.partial(_dn_body, nchunk=tb // DN_CHUNK)
    return pl.pallas_call(
        body,
        grid=(n_batch, nblk),
        in_specs=[fwd(DN_QK), fwd(DN_QK), fwd(D_DN), fwd(LANES),
                  bwd(DN_QK), bwd(DN_QK), bwd(D_DN), bwd(LANES),
                  pl.BlockSpec(tri.shape, lambda b, i: (0, 0, 0))],
        out_specs=[fwd(D_DN), bwd(D_DN)],
        out_shape=[jax.ShapeDtypeStruct((n_batch, t, D_DN), F32)] * 2,
        scratch_shapes=[pltpu.VMEM((2, DN_HEADS, DN_DK, DN_DV), F32)],
        compiler_params=_cparams(("arbitrary", "arbitrary")),
        name="deltanet",
    )(dq, dk, dv, gb, dq, dk, dv, gb, tri)


def _swa_body(sink_ref, q_ref, kp, kc, kn, vp, vc, vn, kx, vx, o_ref, *, ctx, t):
    i = pl.program_id(1)
    bl = SWA_BLOCK
    nband = 3 * bl
    ncol = nband + ctx
    q = q_ref[0]
    qpos = i * bl + lax.broadcasted_iota(jnp.int32, (bl, 1), 0)
    col = lax.broadcasted_iota(jnp.int32, (1, ncol), 1)
    kpos = (i - 1) * bl + col
    band_ok = ((jnp.abs(kpos - qpos) <= SWA_WINDOW) & (kpos >= ctx) & (kpos < t) & (qpos >= ctx)
               & (col < nband))
    mask = band_ok | (col >= nband)
    groups = SWA_Q_HEADS // SWA_KV_HEADS
    outs = []
    for kvh in range(SWA_KV_HEADS):
        sl = slice(kvh * SWA_HD, (kvh + 1) * SWA_HD)
        k_all = jnp.concatenate([kp[0][:, sl], kc[0][:, sl], kn[0][:, sl], kx[0][:, sl]], axis=0)
        v_all = jnp.concatenate([vp[0][:, sl], vc[0][:, sl], vn[0][:, sl], vx[0][:, sl]], axis=0)
        for g in range(groups):
            h = kvh * groups + g
            qh = q[:, h * SWA_HD:(h + 1) * SWA_HD]
            s = lax.dot_general(qh, k_all, (((1,), (1,)), ((), ())), preferred_element_type=F32)
            s = jnp.where(mask, s, NEG_INF)
            sink = sink_ref[h]
            m = jnp.maximum(jnp.max(s, axis=-1, keepdims=True), sink)
            p = jnp.exp(s - m)
            den = jnp.sum(p, axis=-1, keepdims=True) + jnp.exp(sink - m)
            o = jnp.dot(p.astype(BF16), v_all, preferred_element_type=F32)
            outs.append(o / den)
    o_ref[0] = jnp.concatenate(outs, axis=1).astype(BF16)


def _swa(sq, sk, sv, sinks, ctx):
    n_batch, t, _ = sq.shape
    bl = SWA_BLOCK
    nb = t // bl
    body = functools.partial(_swa_body, ctx=ctx, t=t)
    kvspec = lambda f: pl.BlockSpec((1, bl, D_KV), f)
    prev = lambda b, i: (b, jnp.maximum(i - 1, 0), 0)
    cur = lambda b, i: (b, i, 0)
    nxt = lambda b, i: (b, jnp.minimum(i + 1, nb - 1), 0)
    cspec = pl.BlockSpec((1, ctx, D_KV), lambda b, i: (b, 0, 0))
    return pl.pallas_call(
        body,
        grid=(n_batch, nb),
        in_specs=[pl.BlockSpec(memory_space=pltpu.SMEM),
                  pl.BlockSpec((1, bl, D_SWA), cur),
                  kvspec(prev), kvspec(cur), kvspec(nxt),
                  kvspec(prev), kvspec(cur), kvspec(nxt),
                  cspec, cspec],
        out_specs=pl.BlockSpec((1, bl, D_SWA), cur),
        out_shape=jax.ShapeDtypeStruct((n_batch, t, D_SWA), BF16),
        compiler_params=_cparams(("parallel", "parallel")),
        name="swa",
    )(sinks, sq, sk, sk, sk, sv, sv, sv, sk, sv)


def _post_body(of_ref, ob_ref, z_ref, sw_ref, x_ref, mod_ref, og_ref, wo_ref, g2_ref, rw_ref, rb_ref,
               bd_ref, tri_ref, xo_ref, h_ref, route_ref, cnt_ref, *, tm, ctx, n_batch, d):
    b = pl.program_id(0)
    i = pl.program_id(1)
    rows = i * tm + lax.broadcasted_iota(jnp.int32, (tm, 1), 0)
    is_ctx = rows < ctx
    o = of_ref[0] + ob_ref[0]
    ms = _head_sum(o * o, bd_ref[...]) * (1.0 / DN_DV)
    z = z_ref[0]
    dn = o * lax.rsqrt(ms + NORM_EPS) * og_ref[...] * (z * _sigmoid(z))
    mix = jnp.concatenate([dn.astype(BF16), sw_ref[0]], axis=1)
    proj = jnp.dot(mix, wo_ref[...], preferred_element_type=F32)
    gt1 = _mod_chunk(mod_ref, b, n_batch, is_ctx, 2, d)
    x = x_ref[0] + gt1 * proj
    xo_ref[0] = x

    ms2 = jnp.mean(x * x, axis=-1, keepdims=True)
    y = x * lax.rsqrt(ms2 + NORM_EPS) * g2_ref[...]
    sh2 = _mod_chunk(mod_ref, b, n_batch, is_ctx, 3, d)
    sc2 = _mod_chunk(mod_ref, b, n_batch, is_ctx, 4, d)
    h = y * (1.0 + sc2) + sh2
    h_ref[0] = h

    logits = jnp.dot(h, rw_ref[...], precision=HIGHEST, preferred_element_type=F32) + rb_ref[...]
    lane = lax.broadcasted_iota(jnp.int32, (1, LANES), 1)
    vals, idxs = [], []
    l = logits
    for _ in range(TOP_K):
        m = jnp.max(l, axis=-1, keepdims=True)
        idx = jnp.min(jnp.where(l == m, lane, LANES), axis=-1, keepdims=True)
        vals.append(m)
        idxs.append(idx)
        l = jnp.where(lane == idx, -3e38, l)
    es = [jnp.exp(v - vals[0]) for v in vals]
    den = es[0] + es[1] + es[2] + es[3]
    gates = [e / den for e in es]

    @pl.when((b == 0) & (i == 0))
    def _():
        cnt_ref[...] = jnp.zeros_like(cnt_ref)

    onehot = jnp.zeros((tm, LANES), F32)
    for idx in idxs:
        onehot = onehot + (lane == idx).astype(F32)
    run = cnt_ref[0:1, :]
    cum = jnp.dot(tri_ref[...], onehot.astype(BF16), preferred_element_type=F32) + run
    ranks = [jnp.sum(jnp.where(lane == idx, cum, 0.0), axis=-1, keepdims=True) for idx in idxs]
    cnt_ref[...] = jnp.broadcast_to(run + jnp.sum(onehot, axis=0, keepdims=True), cnt_ref.shape)
    route = jnp.zeros((tm, LANES), F32)
    for k in range(TOP_K):
        route = jnp.where(lane == k, idxs[k].astype(F32), route)
        route = jnp.where(lane == TOP_K + k, gates[k], route)
        route = jnp.where(lane == 2 * TOP_K + k, ranks[k], route)
    route_ref[0] = route


def _post(o_f, o_b, pz, o_sw, xu, mod, og, w_out, g2, rw, rb, bd, tri, ctx):
    n_batch, t, d = xu.shape
    tm = TOK_TILE
    body = functools.partial(_post_body, tm=tm, ctx=ctx, n_batch=n_batch, d=d)
    tok = lambda w: pl.BlockSpec((1, tm, w), lambda b, i: (b, i, 0))
    full = lambda a: pl.BlockSpec(a.shape, lambda b, i: (0,) * a.ndim)
    return pl.pallas_call(
        body,
        grid=(n_batch, t // tm),
        in_specs=[tok(D_DN), tok(D_DN), tok(D_DN), tok(D_SWA), tok(d),
                  full(mod), full(og), full(w_out), full(g2), full(rw), full(rb), full(bd), full(tri)],
        out_specs=[tok(d), tok(d), tok(LANES), pl.BlockSpec((SUBLANES, LANES), lambda b, i: (0, 0))],
        out_shape=[jax.ShapeDtypeStruct((n_batch, t, d), F32),
                   jax.ShapeDtypeStruct((n_batch, t, d), F32),
                   jax.ShapeDtypeStruct((n_batch, t, LANES), F32),
                   jax.ShapeDtypeStruct((SUBLANES, LANES), F32)],
        compiler_params=_cparams(("arbitrary", "arbitrary")),
        name="post_attn",
    )(o_f, o_b, pz, o_sw, xu, mod, og, w_out, g2, rw, rb, bd, tri)


def _dispatch_body(dest_ref, h_ref, xs_ref, sem, *, td):
    def issue(tk, carry):
        for k in range(TOP_K):
            dst = dest_ref[0, 0, tk * TOP_K + k]
            pltpu.make_async_copy(h_ref.at[pl.ds(tk, 1)], xs_ref.at[pl.ds(dst, 1)], sem).start()
        return carry

    lax.fori_loop(0, td, issue, 0)

    def drain(tk, carry):
        for k in range(TOP_K):
            pltpu.make_async_copy(h_ref.at[pl.ds(0, 1)], xs_ref.at[pl.ds(0, 1)], sem).wait()
        return carry

    lax.fori_loop(0, td, drain, 0)


def _dispatch(h_flat, dest, n_slots):
    ntok, d = h_flat.shape
    td = TOK_TILE
    body = functools.partial(_dispatch_body, td=td)
    return pl.pallas_call(
        body,
        grid=(ntok // td,),
        in_specs=[pl.BlockSpec((1, 1, td * TOP_K), lambda i: (i, 0, 0), memory_space=pltpu.SMEM),
                  pl.BlockSpec((td, d), lambda i: (i, 0))],
        out_specs=pl.BlockSpec(memory_space=pl.ANY),
        out_shape=jax.ShapeDtypeStruct((n_slots, d), F32),
        scratch_shapes=[pltpu.SemaphoreType.DMA(())],
        compiler_params=_cparams(("arbitrary",)),
        name="moe_dispatch",
    )(dest.reshape(ntok // td, 1, td * TOP_K), h_flat)


def _moe_body(te_ref, nv_ref, nu_ref, xs_ref, wgu_ref, bgu_ref, wdn_ref, bdn_ref, ys_ref, *, tm, de):
    j = pl.program_id(0)

    @pl.when(j < nu_ref[0])
    def _():
        rows = lax.broadcasted_iota(jnp.int32, (tm, 1), 0)
        x = jnp.where(rows < nv_ref[j], xs_ref[...], 0.0).astype(BF16)
        gu = jnp.dot(x, wgu_ref[0], preferred_element_type=F32) + bgu_ref[0]
        gate = jnp.minimum(gu[:, :de], SWIGLU_LIMIT)
        up = jnp.clip(gu[:, de:], -SWIGLU_LIMIT, SWIGLU_LIMIT)
        act = (up + 1.0) * gate * _sigmoid(SWIGLU_ALPHA * gate)
        ys_ref[...] = jnp.dot(act.astype(BF16), wdn_ref[0], preferred_element_type=F32) + bdn_ref[0]

    @pl.when(j >= nu_ref[0])
    def _():
        ys_ref[...] = jnp.zeros_like(ys_ref)


def _moe(xs, tile_expert, tile_valid, n_used, wgu, bgu, wdn, bdn):
    n_slots, d = xs.shape
    tm = MOE_TILE
    n_exp, _, de2 = wgu.shape
    de = de2 // 2
    body = functools.partial(_moe_body, tm=tm, de=de)
    grid_spec = pltpu.PrefetchScalarGridSpec(
        num_scalar_prefetch=3,
        grid=(n_slots // tm,),
        in_specs=[pl.BlockSpec((tm, d), lambda j, te, nv, nu: (j, 0)),
                  pl.BlockSpec((1, d, de2), lambda j, te, nv, nu: (te[j], 0, 0)),
                  pl.BlockSpec((1, 1, de2), lambda j, te, nv, nu: (te[j], 0, 0)),
                  pl.BlockSpec((1, de, d), lambda j, te, nv, nu: (te[j], 0, 0)),
                  pl.BlockSpec((1, 1, d), lambda j, te, nv, nu: (te[j], 0, 0))],
        out_specs=pl.BlockSpec((tm, d), lambda j, te, nv, nu: (j, 0)),
    )
    return pl.pallas_call(
        body,
        grid_spec=grid_spec,
        out_shape=jax.ShapeDtypeStruct((n_slots, d), F32),
        compiler_params=_cparams(("arbitrary",)),
        name="moe_experts",
    )(tile_expert, tile_valid, n_used, xs, wgu, bgu.reshape(n_exp, 1, de2), wdn, bdn.reshape(n_exp, 1, d))


def _combine_body(dest_ref, ys_ref, x_ref, route_ref, mod_ref, o_ref, buf, sem, *, tc, ctx, n_batch, d):
    b = pl.program_id(0)
    i = pl.program_id(1)

    def issue(tk, carry):
        for k in range(TOP_K):
            src = dest_ref[0, 0, tk * TOP_K + k]
            pltpu.make_async_copy(ys_ref.at[pl.ds(src, 1)], buf.at[k, pl.ds(tk, 1)], sem).start()
        return carry

    lax.fori_loop(0, tc, issue, 0)

    def drain(tk, carry):
        for k in range(TOP_K):
            pltpu.make_async_copy(ys_ref.at[pl.ds(0, 1)], buf.at[k, pl.ds(0, 1)], sem).wait()
        return carry

    lax.fori_loop(0, tc, drain, 0)

    route = route_ref[0]
    y = None
    for k in range(TOP_K):
        term = route[:, TOP_K + k:TOP_K + k + 1] * buf[k]
        y = term if y is None else y + term
    rows = i * tc + lax.broadcasted_iota(jnp.int32, (tc, 1), 0)
    gt2 = _mod_chunk(mod_ref, b, n_batch, rows < ctx, 5, d)
    o_ref[0] = x_ref[0] + gt2 * y


def _combine(dest, ys, xu, route, mod, ctx):
    n_batch, t, d = xu.shape
    tc = TOK_TILE
    nt = t // tc
    body = functools.partial(_combine_body, tc=tc, ctx=ctx, n_batch=n_batch, d=d)
    return pl.pallas_call(
        body,
        grid=(n_batch, nt),
        in_specs=[pl.BlockSpec((1, 1, tc * TOP_K), lambda b, i: (b * nt + i, 0, 0), memory_space=pltpu.SMEM),
                  pl.BlockSpec(memory_space=pl.ANY),
                  pl.BlockSpec((1, tc, d), lambda b, i: (b, i, 0)),
                  pl.BlockSpec((1, tc, LANES), lambda b, i: (b, i, 0)),
                  pl.BlockSpec(mod.shape, lambda b, i: (0, 0))],
        out_specs=pl.BlockSpec((1, tc, d), lambda b, i: (b, i, 0)),
        out_shape=jax.ShapeDtypeStruct((n_batch, t, d), F32),
        scratch_shapes=[pltpu.VMEM((TOP_K, tc, d), F32), pltpu.SemaphoreType.DMA(())],
        compiler_params=_cparams(("arbitrary", "arbitrary")),
        name="moe_combine",
    )(dest.reshape(n_batch * nt, 1, tc * TOP_K), ys, xu, route, mod)


def _rope_tables(seq, ctx):
    rows = seq // GRID_W
    row = jnp.repeat(jnp.arange(rows, dtype=F32), GRID_W)
    col = jnp.tile(jnp.arange(GRID_W, dtype=F32), rows)
    inv_freq = jnp.power(ROPE_THETA, -jnp.arange(ROPE_FREQS, dtype=F32) / ROPE_FREQS)
    ang_r = row[:, None] * inv_freq
    ang_c = col[:, None] * inv_freq
    cos = jnp.concatenate([jnp.cos(ang_r), jnp.cos(ang_r), jnp.cos(ang_c), jnp.cos(ang_c)], axis=1)
    sin = jnp.concatenate([-jnp.sin(ang_r), jnp.sin(ang_r), -jnp.sin(ang_c), jnp.sin(ang_c)], axis=1)
    cos = jnp.concatenate([jnp.ones((ctx, SWA_HD), F32), cos], axis=0)
    sin = jnp.concatenate([jnp.zeros((ctx, SWA_HD), F32), sin], axis=0)
    reps = LANES // SWA_HD
    return jnp.tile(cos, (1, reps)), jnp.tile(sin, (1, reps))


def _pad_w_in(w_in):
    sizes = (DN_QK, DN_QK, D_DN, D_DN, 2 * DN_HEADS, 2 * DN_HEADS, D_SWA, D_KV, D_KV)
    offs = np.concatenate([[0], np.cumsum(sizes)])
    part = lambda n: w_in[:, offs[n]:offs[n + 1]]
    pad = jnp.zeros((w_in.shape[0], LANES - 4 * DN_HEADS), w_in.dtype)
    return jnp.concatenate([part(0), part(1), part(2), part(3), part(6), part(7), part(8), part(4), part(5), pad],
                           axis=1).astype(BF16)


def _lane_row(v, fill=0.0):
    out = jnp.full((1, LANES), fill, F32)
    return out.at[0, :v.shape[0]].set(v.astype(F32))


def kernel(x, c, ctx, c_ctx, ada_w, ada_b, norm1_g, w_in, dn_conv_w, dn_a_log, dn_dt_bias, dn_out_g, q_norm_g,
           k_norm_g, sinks, w_out, norm2_g, router_w, router_b, w_gate_up, b_gate_up, w_down, b_down):
    n_batch, seq, d = x.shape
    n_ctx = ctx.shape[1]
    depth = ada_w.shape[0]
    t = n_ctx + seq
    ntok = n_batch * t
    assert n_batch < SUBLANES and n_ctx % TOK_TILE == 0 and seq % TOK_TILE == 0

    xu = jnp.concatenate([ctx, x], axis=1)
    cvec = jnp.zeros((SUBLANES, d), F32).at[:n_batch].set(c).at[n_batch].set(c_ctx)
    cos_t, sin_t = _rope_tables(seq, n_ctx)
    hid = np.arange(DN_QK) // DN_DK
    bd = jnp.asarray(hid[:, None] == hid[None, :], BF16)
    cidx = np.arange(DN_CHUNK)
    tri_dn = jnp.asarray(np.stack([cidx[:, None] >= cidx[None, :], cidx[:, None] <= cidx[None, :]]), BF16)
    ridx = np.arange(TOK_TILE)
    tri_rank = jnp.asarray(ridx[:, None] > ridx[None, :], BF16)

    n_slots = ntok * TOP_K + N_EXPERTS * MOE_TILE
    n_tiles = n_slots // MOE_TILE
    eid = jnp.arange(N_EXPERTS, dtype=jnp.int32)

    for l in range(depth):
        mod = _ada(cvec, ada_w[l], ada_b[l])
        pqkv, pz, psq, pskv, pab = _inproj(xu, mod, norm1_g[l], _pad_w_in(w_in[l]), n_ctx)
        conv_w = jnp.zeros((SUBLANES, dn_conv_w.shape[2]), F32).at[:DN_CONV].set(dn_conv_w[l])
        gate_p = jnp.concatenate([_lane_row(dn_a_log[l].reshape(-1)), _lane_row(dn_dt_bias[l].reshape(-1)),
                                  jnp.zeros((SUBLANES - 2, LANES), F32)], axis=0)
        qg = jnp.tile(q_norm_g[l], SWA_Q_HEADS).reshape(1, D_SWA)
        kg = jnp.tile(k_norm_g[l], SWA_KV_HEADS).reshape(1, D_KV)
        dq, dk, dv, gb, sq, sk, sv = _prep(pqkv, psq, pskv, pab, cos_t, sin_t, conv_w, gate_p, qg, kg, bd, n_ctx)
        o_f, o_b = _deltanet(dq, dk, dv, gb, tri_dn, n_ctx)
        o_sw = _swa(sq, sk, sv, sinks[l], n_ctx)
        og = jnp.tile(dn_out_g[l], DN_HEADS).reshape(1, D_DN)
        rw = jnp.zeros((d, LANES), F32).at[:, :N_EXPERTS].set(router_w[l])
        rb = _lane_row(router_b[l], NEG_INF)
        xu, h, route, cnt = _post(o_f, o_b, pz, o_sw, xu, mod, og, w_out[l].astype(BF16),
                                  norm2_g[l].reshape(1, d), rw, rb, bd, tri_rank, n_ctx)

        route_flat = route.reshape(ntok, LANES)
        idx = route_flat[:, :TOP_K].astype(jnp.int32)
        rank = route_flat[:, 2 * TOP_K:3 * TOP_K].astype(jnp.int32)
        counts = cnt[0, :N_EXPERTS].astype(jnp.int32)
        padded = (counts + MOE_TILE - 1) // MOE_TILE * MOE_TILE
        pad_end = jnp.cumsum(padded)
        pad_start = pad_end - padded
        dest = jnp.sum(jnp.where(idx[:, :, None] == eid[None, None, :], pad_start[None, None, :], 0), axis=-1) + rank
        tile_lo = jnp.arange(n_tiles, dtype=jnp.int32) * MOE_TILE
        tile_expert = jnp.minimum(jnp.sum(pad_end[None, :] <= tile_lo[:, None], axis=1), N_EXPERTS - 1).astype(jnp.int32)
        tile_valid = jnp.clip((pad_start + counts)[tile_expert] - tile_lo, 0, MOE_TILE).astype(jnp.int32)
        n_used = (pad_end[-1:] // MOE_TILE).astype(jnp.int32)

        xs = _dispatch(h.reshape(ntok, d), dest, n_slots)
        ys = _moe(xs, tile_expert, tile_valid, n_used, w_gate_up[l].astype(BF16), b_gate_up[l],
                  w_down[l].astype(BF16), b_down[l])
        xu = _combine(dest, ys, xu, route, mod, n_ctx)

    return xu[:, n_ctx:, :]
```

```python
import functools
import math

import jax
import jax.numpy as jnp
import numpy as np
from jax import lax
from jax.experimental import pallas as pl
from jax.experimental.pallas import tpu as pltpu

F32 = jnp.float32
BF16 = jnp.bfloat16
HIGHEST = lax.Precision.HIGHEST

GRID_W = 64
DN_HEADS = 8
DN_DK = 64
DN_DV = 64
DN_CONV = 5
DN_CHUNK = 64
SWA_Q_HEADS = 8
SWA_KV_HEADS = 2
SWA_HD = 64
SWA_WINDOW = 128
SWA_BLOCK = 128
ROPE_THETA = 10000.0
ROPE_FREQS = SWA_HD // 4
DN_QK = DN_HEADS * DN_DK
D_DN = DN_HEADS * DN_DV
D_SWA = SWA_Q_HEADS * SWA_HD
D_KV = SWA_KV_HEADS * SWA_HD
N_EXPERTS = 32
TOP_K = 4
SWIGLU_LIMIT = 7.0
SWIGLU_ALPHA = 1.702
NORM_EPS = 1e-6
NEG_INF = -1e30

LANES = 128
SUBLANES = 8
TOK_TILE = 256
MOE_TILE = 256
ROW_DMA_UNROLL = 4
VMEM_LIMIT = 48 * 1024 * 1024
MOE_VMEM_LIMIT = 52 * 1024 * 1024


def _sigmoid(x):
    return 1.0 / (1.0 + jnp.exp(-x))


def _cparams(sem, vmem=None):
    return pltpu.CompilerParams(dimension_semantics=sem, vmem_limit_bytes=vmem or VMEM_LIMIT)


def _head_sum(x2, bd):
    hi = x2.astype(BF16)
    lo = (x2 - hi.astype(F32)).astype(BF16)
    return (jnp.dot(hi, bd, preferred_element_type=F32) + jnp.dot(lo, bd, preferred_element_type=F32))


def _mod_chunk(mod_ref, b, n_batch, is_ctx, k, d):
    mb = mod_ref[pl.ds(b, 1), k * d:(k + 1) * d]
    mc = mod_ref[n_batch:n_batch + 1, k * d:(k + 1) * d]
    return jnp.where(is_ctx, mc, mb)


def _ada_body(c_ref, w_ref, b_ref, o_ref):
    c = c_ref[...]
    s = c * _sigmoid(c)
    o_ref[...] = jnp.dot(s, w_ref[0], precision=HIGHEST, preferred_element_type=F32) + b_ref[0]


def _ada(cvec, ada_w, ada_b, layer):
    depth, d, n = ada_w.shape
    tn = n // 4
    return pl.pallas_call(
        _ada_body,
        grid=(n // tn,),
        in_specs=[pl.BlockSpec((SUBLANES, d), lambda j: (0, 0)),
                  pl.BlockSpec((1, d, tn), lambda j: (layer, 0, j)),
                  pl.BlockSpec((1, 1, tn), lambda j: (layer, 0, j))],
        out_specs=pl.BlockSpec((SUBLANES, tn), lambda j: (0, j)),
        out_shape=jax.ShapeDtypeStruct((SUBLANES, n), F32),
        compiler_params=_cparams(("arbitrary",)),
        name="ada_mod",
    )(cvec, ada_w, ada_b.reshape(depth, 1, n))


def _inproj_body(x_ref, mod_ref, g_ref, w_ref, oqkv, oz, osq, oskv, oab, *, tm, ctx, n_batch, d):
    b = pl.program_id(0)
    i = pl.program_id(1)
    x = x_ref[0]
    ms = jnp.mean(x * x, axis=-1, keepdims=True)
    y = x * lax.rsqrt(ms + NORM_EPS) * g_ref[...]
    rows = i * tm + lax.broadcasted_iota(jnp.int32, (tm, 1), 0)
    is_ctx = rows < ctx
    sh = _mod_chunk(mod_ref, b, n_batch, is_ctx, 0, d)
    sc = _mod_chunk(mod_ref, b, n_batch, is_ctx, 1, d)
    h = (y * (1.0 + sc) + sh).astype(BF16)
    p = jnp.dot(h, w_ref[...], preferred_element_type=F32)
    c0 = 2 * DN_QK + D_DN
    oqkv[0] = p[:, :c0]
    oz[0] = p[:, c0:c0 + D_DN]
    c1 = c0 + D_DN
    osq[0] = p[:, c1:c1 + D_SWA]
    c2 = c1 + D_SWA
    oskv[0] = p[:, c2:c2 + 2 * D_KV]
    c3 = c2 + 2 * D_KV
    oab[0] = p[:, c3:c3 + LANES]


def _inproj(xu, mod, g1, w_pad, ctx):
    n_batch, t, d = xu.shape
    tm = TOK_TILE
    npad = w_pad.shape[1]
    widths = (2 * DN_QK + D_DN, D_DN, D_SWA, 2 * D_KV, LANES)
    body = functools.partial(_inproj_body, tm=tm, ctx=ctx, n_batch=n_batch, d=d)
    return pl.pallas_call(
        body,
        grid=(n_batch, t // tm),
        in_specs=[pl.BlockSpec((1, tm, d), lambda b, i: (b, i, 0)),
                  pl.BlockSpec((SUBLANES, 6 * d), lambda b, i: (0, 0)),
                  pl.BlockSpec((1, d), lambda b, i: (0, 0)),
                  pl.BlockSpec((d, npad), lambda b, i: (0, 0))],
        out_specs=[pl.BlockSpec((1, tm, w), lambda b, i: (b, i, 0)) for w in widths],
        out_shape=[jax.ShapeDtypeStruct((n_batch, t, w), F32) for w in widths],
        compiler_params=_cparams(("parallel", "parallel")),
        name="inproj",
    )(xu, mod, g1.reshape(1, d), w_pad)


def _swap16(x):
    lane = lax.broadcasted_iota(jnp.int32, (1, LANES), 1)
    even = ((lane >> 4) & 1) == 0
    nxt = pltpu.roll(x, LANES - 16, 1)
    prv = pltpu.roll(x, 16, 1)
    return jnp.where(even, nxt, prv)


def _rope(x, cos, sin):
    outs = []
    for c in range(x.shape[1] // LANES):
        xc = x[:, c * LANES:(c + 1) * LANES]
        outs.append(xc * cos + _swap16(xc) * sin)
    return outs[0] if len(outs) == 1 else jnp.concatenate(outs, axis=1)


def _prep_body(cur_ref, prv_ref, nxt_ref, sq_ref, skv_ref, ab_ref, cos_ref, sin_ref, cw_ref, gp_ref,
               qg_ref, kg_ref, bd_ref,
               oq, ok, ov, ogb, osq, osk, osv, *, tm, ctx, t):
    i = pl.program_id(1)
    rows = i * tm + lax.broadcasted_iota(jnp.int32, (tm, 1), 0)
    in_ctx = rows < ctx
    seg_lo = jnp.where(in_ctx, 0, ctx)
    seg_hi = jnp.where(in_ctx, ctx, t)
    xe = jnp.concatenate([prv_ref[0], cur_ref[0], nxt_ref[0]], axis=0)
    pad = DN_CONV // 2
    acc = None
    for j in range(DN_CONV):
        xs = xe[SUBLANES - pad + j:SUBLANES - pad + j + tm]
        if j != pad:
            n = rows + (j - pad)
            xs = jnp.where((n >= seg_lo) & (n < seg_hi), xs, 0.0)
        term = xs * cw_ref[j:j + 1, :]
        acc = term if acc is None else acc + term
    y = acc * _sigmoid(acc)
    bd = bd_ref[...]
    q = y[:, :DN_QK]
    k = y[:, DN_QK:2 * DN_QK]
    oq[0] = q * lax.rsqrt(_head_sum(q * q, bd) + NORM_EPS) * (DN_DK ** -0.5)
    ok[0] = k * lax.rsqrt(_head_sum(k * k, bd) + NORM_EPS)
    ov[0] = y[:, 2 * DN_QK:]

    ab = ab_ref[0]
    lane = lax.broadcasted_iota(jnp.int32, (1, LANES), 1)
    z = ab + gp_ref[1:2, :]
    softplus = jnp.maximum(z, 0.0) + jnp.log(1.0 + jnp.exp(-jnp.abs(z)))
    g = -jnp.exp(gp_ref[0:1, :]) * softplus
    ogb[0] = jnp.where(lane < 2 * DN_HEADS, g, _sigmoid(ab))

    cos = cos_ref[...]
    sin = sin_ref[...]
    sq = sq_ref[0]
    qn = sq * lax.rsqrt(_head_sum(sq * sq, bd) * (1.0 / SWA_HD) + NORM_EPS) * qg_ref[...]
    osq[0] = (_rope(qn, cos, sin) * (SWA_HD ** -0.5)).astype(BF16)
    skv = skv_ref[0]
    sk = skv[:, :D_KV]
    kn = sk * lax.rsqrt(_head_sum(sk * sk, bd[:D_KV, :D_KV]) * (1.0 / SWA_HD) + NORM_EPS) * kg_ref[...]
    kr = _rope(kn, cos, sin)
    osk[0] = jnp.concatenate([kr, pltpu.roll(kr, SWA_HD, 1)], axis=1).astype(BF16)
    sv = skv[:, D_KV:]
    osv[0] = jnp.concatenate([sv, pltpu.roll(sv, SWA_HD, 1)], axis=1).astype(BF16)


def _prep(pqkv, psq, pskv, pab, cos_t, sin_t, conv_w, gate_p, qg, kg, bd, ctx):
    n_batch, t, cq = pqkv.shape
    tm = TOK_TILE
    hb = tm // SUBLANES
    nh = t // SUBLANES
    body = functools.partial(_prep_body, tm=tm, ctx=ctx, t=t)
    tok = lambda w: pl.BlockSpec((1, tm, w), lambda b, i: (b, i, 0))
    full = lambda a: pl.BlockSpec(a.shape, lambda b, i: (0,) * a.ndim)
    return pl.pallas_call(
        body,
        grid=(n_batch, t // tm),
        in_specs=[tok(cq),
                  pl.BlockSpec((1, SUBLANES, cq), lambda b, i: (b, jnp.maximum(i * hb - 1, 0), 0)),
                  pl.BlockSpec((1, SUBLANES, cq), lambda b, i: (b, jnp.minimum((i + 1) * hb, nh - 1), 0)),
                  tok(D_SWA), tok(2 * D_KV), tok(LANES),
                  pl.BlockSpec((tm, LANES), lambda b, i: (i, 0)),
                  pl.BlockSpec((tm, LANES), lambda b, i: (i, 0)),
                  full(conv_w), full(gate_p), full(qg), full(kg), full(bd)],
        out_specs=[tok(DN_QK), tok(DN_QK), tok(D_DN), tok(LANES), tok(D_SWA), tok(2 * D_KV), tok(2 * D_KV)],
        out_shape=[jax.ShapeDtypeStruct((n_batch, t, DN_QK), F32),
                   jax.ShapeDtypeStruct((n_batch, t, DN_QK), F32),
                   jax.ShapeDtypeStruct((n_batch, t, D_DN), F32),
                   jax.ShapeDtypeStruct((n_batch, t, LANES), F32),
                   jax.ShapeDtypeStruct((n_batch, t, D_SWA), BF16),
                   jax.ShapeDtypeStruct((n_batch, t, 2 * D_KV), BF16),
                   jax.ShapeDtypeStruct((n_batch, t, 2 * D_KV), BF16)],
        compiler_params=_cparams(("parallel", "parallel")),
        name="prep",
    )(pqkv, pqkv, pqkv, psq, pskv, pab, cos_t, sin_t, conv_w, gate_p, qg, kg, bd)


def _dn_dir_setup(q_ref, k_ref, v_ref, g_ref, tri, row0, rev):
    c = DN_CHUNK
    r = pl.ds(pl.multiple_of(row0, c), c)
    gb = g_ref[0, r, :]
    hi = gb.astype(BF16)
    r1 = gb - hi.astype(F32)
    mid = r1.astype(BF16)
    lo = (r1 - mid.astype(F32)).astype(BF16)
    gc = (jnp.dot(tri, hi, preferred_element_type=F32) + jnp.dot(tri, mid, preferred_element_type=F32)
          + jnp.dot(tri, lo, preferred_element_type=F32))
    last = 0 if rev else c - 1
    gl = gc[last:last + 1, :]
    ii = lax.broadcasted_iota(jnp.int32, (c, LANES), 0)
    jj = lax.broadcasted_iota(jnp.int32, (c, LANES), 1) & (c - 1)
    return dict(r=r, q=q_ref[0, r, :], k=k_ref[0, r, :], v=v_ref[0, r, :], gb=gb, gc=gc, gct=gc.T,
                eg=jnp.exp(gc), kdf=jnp.exp(gl - gc), egl=jnp.exp(gl),
                incl=(ii <= jj) if rev else (ii >= jj), strict=(ii < jj) if rev else (ii > jj),
                eye=(ii == jj).astype(F32))


def _dn_pair(fwd_refs, bwd_refs, of_ref, ob_ref, s_ref, tri_ref, row_f, row_b):
    c = DN_CHUNK
    dirs = [_dn_dir_setup(*fwd_refs, tri_ref[0], row_f, False), _dn_dir_setup(*bwd_refs, tri_ref[1], row_b, True)]
    npair = DN_HEADS // 2
    chains = [(d, p) for d in range(2) for p in range(npair)]
    dot = functools.partial(jnp.dot, preferred_element_type=F32)
    lane = lax.broadcasted_iota(jnp.int32, (1, LANES), 1)
    low = lane < DN_DK
    same_head = ((lax.broadcasted_iota(jnp.int32, (LANES, LANES), 0) < c)
                 == (lax.broadcasted_iota(jnp.int32, (LANES, LANES), 1) < DN_DK))

    def bdiag(m):
        return jnp.where(same_head, jnp.concatenate([m, m], axis=0), 0.0).astype(BF16)

    def tile(name, d, p):
        return dirs[d][name][:, p * LANES:(p + 1) * LANES]

    def col(name, d, p, off=0):
        ln = off + d * DN_HEADS + 2 * p
        a = dirs[d][name]
        return jnp.where(low, a[:, ln:ln + 1], a[:, ln + 1:ln + 2])

    kp = [tile("k", d, p) for d, p in chains]
    qp = [tile("q", d, p) for d, p in chains]
    vp = [tile("v", d, p) for d, p in chains]
    beta = [col("gb", d, p, 2 * DN_HEADS) for d, p in chains]
    egc = [col("eg", d, p) for d, p in chains]
    decay = []
    for d, p in chains:
        ln = d * DN_HEADS + 2 * p
        gct = dirs[d]["gct"]
        grow = jnp.concatenate([gct[ln:ln + 1, :], gct[ln + 1:ln + 2, :]], axis=1)
        incl = dirs[d]["incl"]
        diff = col("gc", d, p) - grow
        decay.append(jnp.where(incl, jnp.exp(jnp.where(incl, diff, 0.0)), 0.0))
    kbp = [k * b for k, b in zip(kp, beta)]
    kq = [lax.dot_general(jnp.concatenate([kb, q], axis=0).astype(BF16), bdiag(k),
                          (((1,), (1,)), ((), ())), preferred_element_type=F32)
          for kb, q, k in zip(kbp, qp, kp)]
    lmat = [jnp.where(dirs[d]["strict"], a[:c] * dc, 0.0) for (d, p), a, dc in zip(chains, kq, decay)]
    aqk = [a[c:] * dc for a, dc in zip(kq, decay)]
    pw = [dot(m.astype(BF16), bdiag(m)) for m in lmat]
    x = [dirs[d]["eye"] - m for (d, p), m in zip(chains, lmat)]
    for _ in range(4):
        r = [dot(jnp.concatenate([xi, pi], axis=0).astype(BF16), bdiag(pi)) for xi, pi in zip(x, pw)]
        x = [xi + ri[:c] for xi, ri in zip(x, r)]
        pw = [ri[c:] for ri in r]
    tinv = [(xi + dot(xi.astype(BF16), bdiag(pi))).astype(BF16) for xi, pi in zip(x, pw)]
    u = [dot(ti, bdiag(v * b)) for ti, v, b in zip(tinv, vp, beta)]
    w = [dot(ti, bdiag(kb * e)) for ti, kb, e in zip(tinv, kbp, egc)]
    kdt = [(k * col("kdf", d, p)).T.astype(BF16) for (d, p), k in zip(chains, kp)]
    s = [s_ref[d, p] for d, p in chains]
    r2 = [dot(jnp.concatenate([wi, q * e], axis=0).astype(BF16), bdiag(si))
          for wi, q, e, si in zip(w, qp, egc, s)]
    vnew = [ui - ri[:c] for ui, ri in zip(u, r2)]
    r3 = [dot(a.astype(BF16), bdiag(vn)) for a, vn in zip(aqk, vnew)]
    sup = [dot(kt, vn.astype(BF16)) for kt, vn in zip(kdt, vnew)]
    for n, (d, p) in enumerate(chains):
        s_ref[d, p] = s[n] * col("egl", d, p) + jnp.where(low, sup[n][:c], sup[n][c:])
    outs = [ri[c:] + r3i for ri, r3i in zip(r2, r3)]
    of_ref[0, dirs[0]["r"], :] = jnp.concatenate(outs[:npair], axis=1)
    ob_ref[0, dirs[1]["r"], :] = jnp.concatenate(outs[npair:], axis=1)


def _dn_body(qf, kf, vf, gf, qb, kb, vb, gbk, tri_ref, of_ref, ob_ref, s_ref, *, nchunk):
    i = pl.program_id(1)

    @pl.when(i == 0)
    def _():
        s_ref[...] = jnp.zeros_like(s_ref)

    def step(cidx, carry):
        _dn_pair((qf, kf, vf, gf), (qb, kb, vb, gbk), of_ref, ob_ref, s_ref, tri_ref,
                 cidx * DN_CHUNK, (nchunk - 1 - cidx) * DN_CHUNK)
        return carry

    lax.fori_loop(0, nchunk, step, 0)


def _deltanet(dq, dk, dv, gb, tri, ctx):
    n_batch, t, _ = dq.shape
    tb = TOK_TILE
    nblk = t // tb
    nctx = ctx // tb

    def bmap(i):
        return jnp.where(i < nctx, nctx - 1 - i, nblk - 1 - (i - nctx))

    fwd = lambda w: pl.BlockSpec((1, tb, w), lambda b, i: (b, i, 0))
    bwd = lambda w: pl.BlockSpec((1, tb, w), lambda b, i: (b, bmap(i), 0))
    body = functools.partial(_dn_body, nchunk=tb // DN_CHUNK)
    return pl.pallas_call(
        body,
        grid=(n_batch, nblk),
        in_specs=[fwd(DN_QK), fwd(DN_QK), fwd(D_DN), fwd(LANES),
                  bwd(DN_QK), bwd(DN_QK), bwd(D_DN), bwd(LANES),
                  pl.BlockSpec(tri.shape, lambda b, i: (0, 0, 0))],
        out_specs=[fwd(D_DN), bwd(D_DN)],
        out_shape=[jax.ShapeDtypeStruct((n_batch, t, D_DN), F32)] * 2,
        scratch_shapes=[pltpu.VMEM((2, DN_HEADS // 2, DN_DK, 2 * DN_DV), F32)],
        compiler_params=_cparams(("arbitrary", "arbitrary")),
        name="deltanet",
    )(dq, dk, dv, gb, dq, dk, dv, gb, tri)


def _swa_body(sink_ref, q_ref, kp, kc, kn, vp, vc, vn, kx, vx, o_ref, *, ctx, t):
    i = pl.program_id(1)
    bl = SWA_BLOCK
    nband = 3 * bl
    ncol = nband + ctx
    q = q_ref[0]
    row2 = lax.broadcasted_iota(jnp.int32, (2 * bl, 1), 0)
    qpos = i * bl + (row2 & (bl - 1))
    col = lax.broadcasted_iota(jnp.int32, (1, ncol), 1)
    kpos = (i - 1) * bl + col
    band_ok = ((jnp.abs(kpos - qpos) <= SWA_WINDOW) & (kpos >= ctx) & (kpos < t) & (qpos >= ctx)
               & (col < nband))
    mask = band_ok | (col >= nband)
    k2 = jnp.concatenate([kp[0], kc[0], kn[0], kx[0]], axis=0)
    v2 = jnp.concatenate([vp[0], vc[0], vn[0], vx[0]], axis=0)
    low = lax.broadcasted_iota(jnp.int32, (ncol, LANES), 1) < SWA_HD
    zero = jnp.zeros((ncol, LANES), BF16)
    pick = lambda a: [jnp.where(low, a[:, :LANES], zero), jnp.where(low, zero, a[:, LANES:]),
                      jnp.where(low, a[:, LANES:], zero), jnp.where(low, zero, a[:, :LANES])]
    kvar = pick(k2)
    vvar = pick(v2)
    pairs = D_SWA // LANES
    qg = [jnp.concatenate([q[:, (2 * g) * LANES:(2 * g + 1) * LANES],
                           q[:, (2 * g + 1) * LANES:(2 * g + 2) * LANES]], axis=0) for g in range(pairs // 2)]
    cases = [(g, par) for g in range(pairs // 2) for par in range(2)]
    scores = [lax.dot_general(qg[g], kvar[2 * g + par], (((1,), (1,)), ((), ())), preferred_element_type=F32)
              for g, par in cases]
    probs, rden = [], []
    for (g, par), s in zip(cases, scores):
        h_top = 4 * g + par
        sink = jnp.where(row2 < bl, sink_ref[h_top], sink_ref[h_top + 2])
        s = jnp.where(mask, s, NEG_INF)
        m = jnp.maximum(jnp.max(s, axis=-1, keepdims=True), sink)
        p = jnp.exp(s - m)
        rden.append(1.0 / (jnp.sum(p, axis=-1, keepdims=True) + jnp.exp(sink - m)))
        probs.append(p.astype(BF16))
    pv = [jnp.dot(p, vvar[2 * g + par], preferred_element_type=F32) * r
          for (g, par), p, r in zip(cases, probs, rden)]
    outs = []
    for g in range(pairs // 2):
        og = pv[2 * g] + pv[2 * g + 1]
        outs += [og[:bl], og[bl:]]
    o_ref[0] = jnp.concatenate(outs, axis=1).astype(BF16)


def _swa(sq, sk, sv, sinks, ctx):
    n_batch, t, _ = sq.shape
    bl = SWA_BLOCK
    nb = t // bl
    body = functools.partial(_swa_body, ctx=ctx, t=t)
    kvspec = lambda f: pl.BlockSpec((1, bl, 2 * D_KV), f)
    prev = lambda b, i: (b, jnp.maximum(i - 1, 0), 0)
    cur = lambda b, i: (b, i, 0)
    nxt = lambda b, i: (b, jnp.minimum(i + 1, nb - 1), 0)
    cspec = pl.BlockSpec((1, ctx, 2 * D_KV), lambda b, i: (b, 0, 0))
    return pl.pallas_call(
        body,
        grid=(n_batch, nb),
        in_specs=[pl.BlockSpec(memory_space=pltpu.SMEM),
                  pl.BlockSpec((1, bl, D_SWA), cur),
                  kvspec(prev), kvspec(cur), kvspec(nxt),
                  kvspec(prev), kvspec(cur), kvspec(nxt),
                  cspec, cspec],
        out_specs=pl.BlockSpec((1, bl, D_SWA), cur),
        out_shape=jax.ShapeDtypeStruct((n_batch, t, D_SWA), BF16),
        compiler_params=_cparams(("parallel", "parallel")),
        name="swa",
    )(sinks, sq, sk, sk, sk, sv, sv, sv, sk, sv)


def _post_body(of_ref, ob_ref, z_ref, sw_ref, x_ref, mod_ref, og_ref, wo_ref, g2_ref, rw_ref, rb_ref,
               bd_ref, tri_ref, xo_ref, h_ref, route_ref, cnt_ref, *, tm, ctx, n_batch, d):
    b = pl.program_id(0)
    i = pl.program_id(1)
    rows = i * tm + lax.broadcasted_iota(jnp.int32, (tm, 1), 0)
    is_ctx = rows < ctx
    o = of_ref[0] + ob_ref[0]
    ms = _head_sum(o * o, bd_ref[...]) * (1.0 / DN_DV)
    z = z_ref[0]
    dn = o * lax.rsqrt(ms + NORM_EPS) * og_ref[...] * (z * _sigmoid(z))
    mix = jnp.concatenate([dn.astype(BF16), sw_ref[0]], axis=1)
    proj = jnp.dot(mix, wo_ref[...], preferred_element_type=F32)
    gt1 = _mod_chunk(mod_ref, b, n_batch, is_ctx, 2, d)
    x = x_ref[0] + gt1 * proj
    xo_ref[0] = x

    ms2 = jnp.mean(x * x, axis=-1, keepdims=True)
    y = x * lax.rsqrt(ms2 + NORM_EPS) * g2_ref[...]
    sh2 = _mod_chunk(mod_ref, b, n_batch, is_ctx, 3, d)
    sc2 = _mod_chunk(mod_ref, b, n_batch, is_ctx, 4, d)
    h = y * (1.0 + sc2) + sh2
    h_ref[0] = h

    h_hi = h.astype(BF16)
    h_lo = (h - h_hi.astype(F32)).astype(BF16)
    logits = (jnp.dot(h_hi, rw_ref[0], preferred_element_type=F32)
              + jnp.dot(h_lo, rw_ref[0], preferred_element_type=F32)
              + jnp.dot(h_hi, rw_ref[1], preferred_element_type=F32)
              + jnp.dot(h_lo, rw_ref[1], preferred_element_type=F32)) + rb_ref[...]
    lane = lax.broadcasted_iota(jnp.int32, (1, LANES), 1)
    vals, idxs = [], []
    l = logits
    for _ in range(TOP_K):
        m = jnp.max(l, axis=-1, keepdims=True)
        idx = jnp.min(jnp.where(l == m, lane, LANES), axis=-1, keepdims=True)
        vals.append(m)
        idxs.append(idx)
        l = jnp.where(lane == idx, -3e38, l)
    es = [jnp.exp(v - vals[0]) for v in vals]
    den = es[0] + es[1] + es[2] + es[3]
    gates = [e / den for e in es]

    @pl.when((b == 0) & (i == 0))
    def _():
        cnt_ref[...] = jnp.zeros_like(cnt_ref)

    onehot = jnp.zeros((tm, LANES), F32)
    for idx in idxs:
        onehot = onehot + (lane == idx).astype(F32)
    run = cnt_ref[0:1, :]
    cum = jnp.dot(tri_ref[...], onehot.astype(BF16), preferred_element_type=F32) + run
    ranks = [jnp.sum(jnp.where(lane == idx, cum, 0.0), axis=-1, keepdims=True) for idx in idxs]
    cnt_ref[...] = jnp.broadcast_to(run + jnp.sum(onehot, axis=0, keepdims=True), cnt_ref.shape)
    route = jnp.zeros((tm, LANES), F32)
    for k in range(TOP_K):
        route = jnp.where(lane == k, idxs[k].astype(F32), route)
        route = jnp.where(lane == TOP_K + k, gates[k], route)
        route = jnp.where(lane == 2 * TOP_K + k, ranks[k], route)
    route_ref[0] = route


def _post(o_f, o_b, pz, o_sw, xu, mod, og, w_out, g2, rw, rb, bd, tri, ctx):
    n_batch, t, d = xu.shape
    tm = TOK_TILE
    body = functools.partial(_post_body, tm=tm, ctx=ctx, n_batch=n_batch, d=d)
    tok = lambda w: pl.BlockSpec((1, tm, w), lambda b, i: (b, i, 0))
    full = lambda a: pl.BlockSpec(a.shape, lambda b, i: (0,) * a.ndim)
    return pl.pallas_call(
        body,
        grid=(n_batch, t // tm),
        in_specs=[tok(D_DN), tok(D_DN), tok(D_DN), tok(D_SWA), tok(d),
                  full(mod), full(og), full(w_out), full(g2), full(rw), full(rb), full(bd), full(tri)],
        out_specs=[tok(d), tok(d), tok(LANES), pl.BlockSpec((SUBLANES, LANES), lambda b, i: (0, 0))],
        out_shape=[jax.ShapeDtypeStruct((n_batch, t, d), F32),
                   jax.ShapeDtypeStruct((n_batch, t, d), F32),
                   jax.ShapeDtypeStruct((n_batch, t, LANES), F32),
                   jax.ShapeDtypeStruct((SUBLANES, LANES), F32)],
        compiler_params=_cparams(("arbitrary", "arbitrary")),
        name="post_attn",
    )(o_f, o_b, pz, o_sw, xu, mod, og, w_out, g2, rw, rb, bd, tri)


def _dispatch_body(dest_ref, h_ref, xs_ref, sem, *, td):
    def issue(n, carry):
        for u in range(ROW_DMA_UNROLL):
            tk = n * ROW_DMA_UNROLL + u
            for k in range(TOP_K):
                dst = dest_ref[0, 0, tk * TOP_K + k]
                pltpu.make_async_copy(h_ref.at[pl.ds(tk, 1)], xs_ref.at[pl.ds(dst, 1)], sem).start(priority=k % 2)
        return carry

    lax.fori_loop(0, td // ROW_DMA_UNROLL, issue, 0)

    def drain(n, carry):
        for _ in range(ROW_DMA_UNROLL * TOP_K):
            pltpu.make_async_copy(h_ref.at[pl.ds(0, 1)], xs_ref.at[pl.ds(0, 1)], sem).wait()
        return carry

    lax.fori_loop(0, td // ROW_DMA_UNROLL, drain, 0)


def _dispatch(h_flat, dest, n_slots):
    ntok, d = h_flat.shape
    td = TOK_TILE
    body = functools.partial(_dispatch_body, td=td)
    return pl.pallas_call(
        body,
        grid=(ntok // td,),
        in_specs=[pl.BlockSpec((1, 1, td * TOP_K), lambda i: (i, 0, 0), memory_space=pltpu.SMEM),
                  pl.BlockSpec((td, d), lambda i: (i, 0))],
        out_specs=pl.BlockSpec(memory_space=pl.ANY),
        out_shape=jax.ShapeDtypeStruct((n_slots, d), F32),
        scratch_shapes=[pltpu.SemaphoreType.DMA(())],
        compiler_params=_cparams(("arbitrary",)),
        name="moe_dispatch",
    )(dest.reshape(ntok // td, 1, td * TOP_K), h_flat)


def _cast_rows(src_ref, dst_ref, rows_per_step):
    def step(n, carry):
        r = pl.ds(pl.multiple_of(n * rows_per_step, rows_per_step), rows_per_step)
        dst_ref[r, :] = src_ref[0, 0, r, :].astype(BF16)
        return carry

    lax.fori_loop(0, src_ref.shape[2] // rows_per_step, step, 0)


def _moe_body(te_ref, nv_ref, nu_ref, xs_ref, wgu_ref, bgu_ref, wdn_ref, bdn_ref, ys_ref, wgu_bf, wdn_bf, *, tm, de):
    j = pl.program_id(0)
    used = j < nu_ref[0]
    prev = te_ref[jnp.maximum(j - 1, 0)]

    @pl.when(used & ((j == 0) | (te_ref[j] != prev)))
    def _():
        _cast_rows(wgu_ref, wgu_bf, LANES)
        _cast_rows(wdn_ref, wdn_bf, LANES)

    @pl.when(used)
    def _():
        rows = lax.broadcasted_iota(jnp.int32, (tm, 1), 0)
        x = jnp.where(rows < nv_ref[j], xs_ref[...], 0.0).astype(BF16)
        gu = jnp.dot(x, wgu_bf[...], preferred_element_type=F32) + bgu_ref[0, 0]
        gate = jnp.minimum(gu[:, :de], SWIGLU_LIMIT)
        up = jnp.clip(gu[:, de:], -SWIGLU_LIMIT, SWIGLU_LIMIT)
        act = (up + 1.0) * gate * _sigmoid(SWIGLU_ALPHA * gate)
        ys_ref[...] = jnp.dot(act.astype(BF16), wdn_bf[...], preferred_element_type=F32) + bdn_ref[0, 0]

    @pl.when(jnp.logical_not(used))
    def _():
        ys_ref[...] = jnp.zeros_like(ys_ref)


def _moe(xs, tile_expert, tile_valid, n_used, wgu, bgu, wdn, bdn, layer):
    n_slots, d = xs.shape
    tm = MOE_TILE
    _, n_exp, _, de2 = wgu.shape
    de = de2 // 2
    body = functools.partial(_moe_body, tm=tm, de=de)
    wmap = lambda j, te, nv, nu: (layer, te[j], 0, 0)
    grid_spec = pltpu.PrefetchScalarGridSpec(
        num_scalar_prefetch=3,
        grid=(n_slots // tm,),
        in_specs=[pl.BlockSpec((tm, d), lambda j, te, nv, nu: (j, 0)),
                  pl.BlockSpec((1, 1, d, de2), wmap),
                  pl.BlockSpec((1, 1, 1, de2), wmap),
                  pl.BlockSpec((1, 1, de, d), wmap),
                  pl.BlockSpec((1, 1, 1, d), wmap)],
        out_specs=pl.BlockSpec((tm, d), lambda j, te, nv, nu: (j, 0)),
        scratch_shapes=[pltpu.VMEM((d, de2), BF16), pltpu.VMEM((de, d), BF16)],
    )
    depth = wgu.shape[0]
    return pl.pallas_call(
        body,
        grid_spec=grid_spec,
        out_shape=jax.ShapeDtypeStruct((n_slots, d), F32),
        compiler_params=_cparams(("arbitrary",), MOE_VMEM_LIMIT),
        name="moe_experts",
    )(tile_expert, tile_valid, n_used, xs, wgu, bgu.reshape(depth, n_exp, 1, de2), wdn,
      bdn.reshape(depth, n_exp, 1, d))


def _combine_body(dest_ref, ys_ref, x_ref, route_ref, mod_ref, o_ref, buf, sem, *, tc, ctx, n_batch, d):
    b = pl.program_id(0)
    i = pl.program_id(1)

    def issue(n, carry):
        for u in range(ROW_DMA_UNROLL):
            tk = n * ROW_DMA_UNROLL + u
            for k in range(TOP_K):
                src = dest_ref[0, 0, tk * TOP_K + k]
                pltpu.make_async_copy(ys_ref.at[pl.ds(src, 1)], buf.at[k, pl.ds(tk, 1)], sem).start(priority=k % 2)
        return carry

    lax.fori_loop(0, tc // ROW_DMA_UNROLL, issue, 0)

    def drain(n, carry):
        for _ in range(ROW_DMA_UNROLL * TOP_K):
            pltpu.make_async_copy(ys_ref.at[pl.ds(0, 1)], buf.at[0, pl.ds(0, 1)], sem).wait()
        return carry

    lax.fori_loop(0, tc // ROW_DMA_UNROLL, drain, 0)

    route = route_ref[0]
    y = None
    for k in range(TOP_K):
        term = route[:, TOP_K + k:TOP_K + k + 1] * buf[k]
        y = term if y is None else y + term
    rows = i * tc + lax.broadcasted_iota(jnp.int32, (tc, 1), 0)
    gt2 = _mod_chunk(mod_ref, b, n_batch, rows < ctx, 5, d)
    o_ref[0] = x_ref[0] + gt2 * y


def _combine(dest, ys, xu, route, mod, ctx):
    n_batch, t, d = xu.shape
    tc = TOK_TILE
    nt = t // tc
    body = functools.partial(_combine_body, tc=tc, ctx=ctx, n_batch=n_batch, d=d)
    return pl.pallas_call(
        body,
        grid=(n_batch, nt),
        in_specs=[pl.BlockSpec((1, 1, tc * TOP_K), lambda b, i: (b * nt + i, 0, 0), memory_space=pltpu.SMEM),
                  pl.BlockSpec(memory_space=pl.ANY),
                  pl.BlockSpec((1, tc, d), lambda b, i: (b, i, 0)),
                  pl.BlockSpec((1, tc, LANES), lambda b, i: (b, i, 0)),
                  pl.BlockSpec(mod.shape, lambda b, i: (0, 0))],
        out_specs=pl.BlockSpec((1, tc, d), lambda b, i: (b, i, 0)),
        out_shape=jax.ShapeDtypeStruct((n_batch, t, d), F32),
        scratch_shapes=[pltpu.VMEM((TOP_K, tc, d), F32), pltpu.SemaphoreType.DMA(())],
        compiler_params=_cparams(("arbitrary", "arbitrary")),
        name="moe_combine",
    )(dest.reshape(n_batch * nt, 1, tc * TOP_K), ys, xu, route, mod)


def _rope_tables(seq, ctx):
    rows = seq // GRID_W
    row = jnp.repeat(jnp.arange(rows, dtype=F32), GRID_W)
    col = jnp.tile(jnp.arange(GRID_W, dtype=F32), rows)
    inv_freq = jnp.power(ROPE_THETA, -jnp.arange(ROPE_FREQS, dtype=F32) / ROPE_FREQS)
    ang_r = row[:, None] * inv_freq
    ang_c = col[:, None] * inv_freq
    cos = jnp.concatenate([jnp.cos(ang_r), jnp.cos(ang_r), jnp.cos(ang_c), jnp.cos(ang_c)], axis=1)
    sin = jnp.concatenate([-jnp.sin(ang_r), jnp.sin(ang_r), -jnp.sin(ang_c), jnp.sin(ang_c)], axis=1)
    cos = jnp.concatenate([jnp.ones((ctx, SWA_HD), F32), cos], axis=0)
    sin = jnp.concatenate([jnp.zeros((ctx, SWA_HD), F32), sin], axis=0)
    reps = LANES // SWA_HD
    return jnp.tile(cos, (1, reps)), jnp.tile(sin, (1, reps))


def _pad_w_in(w_in):
    sizes = (DN_QK, DN_QK, D_DN, D_DN, 2 * DN_HEADS, 2 * DN_HEADS, D_SWA, D_KV, D_KV)
    offs = np.concatenate([[0], np.cumsum(sizes)])
    part = lambda n: w_in[:, offs[n]:offs[n + 1]]
    pad = jnp.zeros((w_in.shape[0], LANES - 4 * DN_HEADS), w_in.dtype)
    return jnp.concatenate([part(0), part(1), part(2), part(3), part(6), part(7), part(8), part(4), part(5), pad],
                           axis=1).astype(BF16)


def _lane_row(v, fill=0.0):
    out = jnp.full((1, LANES), fill, F32)
    return out.at[0, :v.shape[0]].set(v.astype(F32))


def kernel(x, c, ctx, c_ctx, ada_w, ada_b, norm1_g, w_in, dn_conv_w, dn_a_log, dn_dt_bias, dn_out_g, q_norm_g,
           k_norm_g, sinks, w_out, norm2_g, router_w, router_b, w_gate_up, b_gate_up, w_down, b_down):
    n_batch, seq, d = x.shape
    n_ctx = ctx.shape[1]
    depth = ada_w.shape[0]
    t = n_ctx + seq
    ntok = n_batch * t
    assert n_batch < SUBLANES and n_ctx % TOK_TILE == 0 and seq % TOK_TILE == 0

    xu = jnp.concatenate([ctx, x], axis=1)
    cvec = jnp.zeros((SUBLANES, d), F32).at[:n_batch].set(c).at[n_batch].set(c_ctx)
    cos_t, sin_t = _rope_tables(seq, n_ctx)
    hid = np.arange(DN_QK) // DN_DK
    bd = jnp.asarray(hid[:, None] == hid[None, :], BF16)
    cidx = np.arange(DN_CHUNK)
    tri_dn = jnp.asarray(np.stack([cidx[:, None] >= cidx[None, :], cidx[:, None] <= cidx[None, :]]), BF16)
    ridx = np.arange(TOK_TILE)
    tri_rank = jnp.asarray(ridx[:, None] > ridx[None, :], BF16)

    n_slots = ntok * TOP_K + N_EXPERTS * MOE_TILE
    n_tiles = n_slots // MOE_TILE
    eid = jnp.arange(N_EXPERTS, dtype=jnp.int32)

    for l in range(depth):
        mod = _ada(cvec, ada_w, ada_b, l)
        pqkv, pz, psq, pskv, pab = _inproj(xu, mod, norm1_g[l], _pad_w_in(w_in[l]), n_ctx)
        conv_w = jnp.zeros((SUBLANES, dn_conv_w.shape[2]), F32).at[:DN_CONV].set(dn_conv_w[l])
        gate_p = jnp.concatenate([_lane_row(dn_a_log[l].reshape(-1)), _lane_row(dn_dt_bias[l].reshape(-1)),
                                  jnp.zeros((SUBLANES - 2, LANES), F32)], axis=0)
        qg = jnp.tile(q_norm_g[l], SWA_Q_HEADS).reshape(1, D_SWA)
        kg = jnp.tile(k_norm_g[l], SWA_KV_HEADS).reshape(1, D_KV)
        dq, dk, dv, gb, sq, sk, sv = _prep(pqkv, psq, pskv, pab, cos_t, sin_t, conv_w, gate_p, qg, kg, bd, n_ctx)
        o_f, o_b = _deltanet(dq, dk, dv, gb, tri_dn, n_ctx)
        o_sw = _swa(sq, sk, sv, sinks[l], n_ctx)
        og = jnp.tile(dn_out_g[l], DN_HEADS).reshape(1, D_DN)
        rw32 = jnp.zeros((d, LANES), F32).at[:, :N_EXPERTS].set(router_w[l])
        rw_hi = rw32.astype(BF16)
        rw = jnp.stack([rw_hi, (rw32 - rw_hi.astype(F32)).astype(BF16)])
        rb = _lane_row(router_b[l], NEG_INF)
        xu, h, route, cnt = _post(o_f, o_b, pz, o_sw, xu, mod, og, w_out[l].astype(BF16),
                                  norm2_g[l].reshape(1, d), rw, rb, bd, tri_rank, n_ctx)

        route_flat = route.reshape(ntok, LANES)
        idx = route_flat[:, :TOP_K].astype(jnp.int32)
        rank = route_flat[:, 2 * TOP_K:3 * TOP_K].astype(jnp.int32)
        counts = cnt[0, :N_EXPERTS].astype(jnp.int32)
        padded = (counts + MOE_TILE - 1) // MOE_TILE * MOE_TILE
        pad_end = jnp.cumsum(padded)
        pad_start = pad_end - padded
        dest = jnp.sum(jnp.where(idx[:, :, None] == eid[None, None, :], pad_start[None, None, :], 0), axis=-1) + rank
        tile_lo = jnp.arange(n_tiles, dtype=jnp.int32) * MOE_TILE
        tile_expert = jnp.minimum(jnp.sum(pad_end[None, :] <= tile_lo[:, None], axis=1), N_EXPERTS - 1).astype(jnp.int32)
        tile_valid = jnp.clip((pad_start + counts)[tile_expert] - tile_lo, 0, MOE_TILE).astype(jnp.int32)
        n_used = (pad_end[-1:] // MOE_TILE).astype(jnp.int32)

        xs = _dispatch(h.reshape(ntok, d), dest, n_slots)
        ys = _moe(xs, tile_expert, tile_valid, n_used, w_gate_up, b_gate_up, w_down, b_down, l)
        xu = _combine(dest, ys, xu, route, mod, n_ctx)

    return xu[:, n_ctx:, :]
```

```python
import functools
import math

import jax
import jax.numpy as jnp
import numpy as np
from jax import lax
from jax.experimental import pallas as pl
from jax.experimental.pallas import tpu as pltpu

F32 = jnp.float32
BF16 = jnp.bfloat16
HIGHEST = lax.Precision.HIGHEST

GRID_W = 64
DN_HEADS = 8
DN_DK = 64
DN_DV = 64
DN_CONV = 5
DN_CHUNK = 64
SWA_Q_HEADS = 8
SWA_KV_HEADS = 2
SWA_HD = 64
SWA_WINDOW = 128
SWA_BLOCK = 128
ROPE_THETA = 10000.0
ROPE_FREQS = SWA_HD // 4
DN_QK = DN_HEADS * DN_DK
D_DN = DN_HEADS * DN_DV
D_SWA = SWA_Q_HEADS * SWA_HD
D_KV = SWA_KV_HEADS * SWA_HD
N_EXPERTS = 32
TOP_K = 4
SWIGLU_LIMIT = 7.0
SWIGLU_ALPHA = 1.702
NORM_EPS = 1e-6
NEG_INF = -1e30

LANES = 128
SUBLANES = 8
TOK_TILE = 256
MOE_TILE = 256
DN_INV_BASE_LOG2 = 3
ROW_DMA_UNROLL = 4
VMEM_LIMIT = 48 * 1024 * 1024
MOE_VMEM_LIMIT = 52 * 1024 * 1024


def _sigmoid(x):
    return 1.0 / (1.0 + jnp.exp(-x))


def _cparams(sem, vmem=None):
    return pltpu.CompilerParams(dimension_semantics=sem, vmem_limit_bytes=vmem or VMEM_LIMIT)


def _head_sum(x2, bd):
    hi = x2.astype(BF16)
    lo = (x2 - hi.astype(F32)).astype(BF16)
    return (jnp.dot(hi, bd, preferred_element_type=F32) + jnp.dot(lo, bd, preferred_element_type=F32))


def _mod_chunk(mod_ref, b, n_batch, is_ctx, k, d):
    mb = mod_ref[pl.ds(b, 1), k * d:(k + 1) * d]
    mc = mod_ref[n_batch:n_batch + 1, k * d:(k + 1) * d]
    return jnp.where(is_ctx, mc, mb)


def _ada_body(c_ref, w_ref, b_ref, o_ref):
    c = c_ref[...]
    s = c * _sigmoid(c)
    o_ref[...] = jnp.dot(s, w_ref[0], precision=HIGHEST, preferred_element_type=F32) + b_ref[0]


def _ada(cvec, ada_w, ada_b, layer):
    depth, d, n = ada_w.shape
    tn = n // 4
    return pl.pallas_call(
        _ada_body,
        grid=(n // tn,),
        in_specs=[pl.BlockSpec((SUBLANES, d), lambda j: (0, 0)),
                  pl.BlockSpec((1, d, tn), lambda j: (layer, 0, j)),
                  pl.BlockSpec((1, 1, tn), lambda j: (layer, 0, j))],
        out_specs=pl.BlockSpec((SUBLANES, tn), lambda j: (0, j)),
        out_shape=jax.ShapeDtypeStruct((SUBLANES, n), F32),
        compiler_params=_cparams(("arbitrary",)),
        name="ada_mod",
    )(cvec, ada_w, ada_b.reshape(depth, 1, n))


def _inproj_body(x_ref, mod_ref, g_ref, w_ref, oqkv, oz, osq, oskv, oab, *, tm, ctx, n_batch, d):
    b = pl.program_id(0)
    i = pl.program_id(1)
    x = x_ref[0]
    ms = jnp.mean(x * x, axis=-1, keepdims=True)
    y = x * lax.rsqrt(ms + NORM_EPS) * g_ref[...]
    rows = i * tm + lax.broadcasted_iota(jnp.int32, (tm, 1), 0)
    is_ctx = rows < ctx
    sh = _mod_chunk(mod_ref, b, n_batch, is_ctx, 0, d)
    sc = _mod_chunk(mod_ref, b, n_batch, is_ctx, 1, d)
    h = (y * (1.0 + sc) + sh).astype(BF16)
    p = jnp.dot(h, w_ref[...], preferred_element_type=F32)
    c0 = 2 * DN_QK + D_DN
    oqkv[0] = p[:, :c0]
    oz[0] = p[:, c0:c0 + D_DN]
    c1 = c0 + D_DN
    osq[0] = p[:, c1:c1 + D_SWA]
    c2 = c1 + D_SWA
    oskv[0] = p[:, c2:c2 + 2 * D_KV]
    c3 = c2 + 2 * D_KV
    oab[0] = p[:, c3:c3 + LANES]


def _inproj(xu, mod, g1, w_pad, ctx):
    n_batch, t, d = xu.shape
    tm = TOK_TILE
    npad = w_pad.shape[1]
    widths = (2 * DN_QK + D_DN, D_DN, D_SWA, 2 * D_KV, LANES)
    body = functools.partial(_inproj_body, tm=tm, ctx=ctx, n_batch=n_batch, d=d)
    return pl.pallas_call(
        body,
        grid=(n_batch, t // tm),
        in_specs=[pl.BlockSpec((1, tm, d), lambda b, i: (b, i, 0)),
                  pl.BlockSpec((SUBLANES, 6 * d), lambda b, i: (0, 0)),
                  pl.BlockSpec((1, d), lambda b, i: (0, 0)),
                  pl.BlockSpec((d, npad), lambda b, i: (0, 0))],
        out_specs=[pl.BlockSpec((1, tm, w), lambda b, i: (b, i, 0)) for w in widths],
        out_shape=[jax.ShapeDtypeStruct((n_batch, t, w), F32) for w in widths],
        compiler_params=_cparams(("parallel", "parallel")),
        name="inproj",
    )(xu, mod, g1.reshape(1, d), w_pad)


def _swap16(x):
    lane = lax.broadcasted_iota(jnp.int32, (1, LANES), 1)
    even = ((lane >> 4) & 1) == 0
    nxt = pltpu.roll(x, LANES - 16, 1)
    prv = pltpu.roll(x, 16, 1)
    return jnp.where(even, nxt, prv)


def _rope(x, cos, sin):
    outs = []
    for c in range(x.shape[1] // LANES):
        xc = x[:, c * LANES:(c + 1) * LANES]
        outs.append(xc * cos + _swap16(xc) * sin)
    return outs[0] if len(outs) == 1 else jnp.concatenate(outs, axis=1)


def _prep_body(cur_ref, prv_ref, nxt_ref, sq_ref, skv_ref, ab_ref, cos_ref, sin_ref, cw_ref, gp_ref,
               qg_ref, kg_ref, bd_ref,
               oq, ok, ov, ogb, osq, osk, osv, *, tm, ctx, t):
    i = pl.program_id(1)
    lo = i * tm
    hi = lo + tm
    prv = jnp.where((lo == 0) | (lo == ctx), 0.0, prv_ref[0])
    nxt = jnp.where((hi == ctx) | (hi == t), 0.0, nxt_ref[0])
    xe = jnp.concatenate([prv, cur_ref[0], nxt], axis=0)
    pad = DN_CONV // 2
    acc = None
    for j in range(DN_CONV):
        term = xe[SUBLANES - pad + j:SUBLANES - pad + j + tm] * cw_ref[j:j + 1, :]
        acc = term if acc is None else acc + term
    y = acc * _sigmoid(acc)
    bd = bd_ref[...]
    q = y[:, :DN_QK]
    k = y[:, DN_QK:2 * DN_QK]
    oq[0] = q * lax.rsqrt(_head_sum(q * q, bd) + NORM_EPS) * (DN_DK ** -0.5)
    ok[0] = k * lax.rsqrt(_head_sum(k * k, bd) + NORM_EPS)
    ov[0] = y[:, 2 * DN_QK:]

    ab = ab_ref[0]
    lane = lax.broadcasted_iota(jnp.int32, (1, LANES), 1)
    z = ab + gp_ref[1:2, :]
    softplus = jnp.maximum(z, 0.0) + jnp.log(1.0 + jnp.exp(-jnp.abs(z)))
    g = -jnp.exp(gp_ref[0:1, :]) * softplus
    ogb[0] = jnp.where(lane < 2 * DN_HEADS, g, _sigmoid(ab))

    cos = cos_ref[...]
    sin = sin_ref[...]
    sq = sq_ref[0]
    qn = sq * lax.rsqrt(_head_sum(sq * sq, bd) * (1.0 / SWA_HD) + NORM_EPS) * qg_ref[...]
    osq[0] = (_rope(qn, cos, sin) * (SWA_HD ** -0.5)).astype(BF16)
    skv = skv_ref[0]
    sk = skv[:, :D_KV]
    kn = sk * lax.rsqrt(_head_sum(sk * sk, bd[:D_KV, :D_KV]) * (1.0 / SWA_HD) + NORM_EPS) * kg_ref[...]
    kr = _rope(kn, cos, sin)
    osk[0] = jnp.concatenate([kr, pltpu.roll(kr, SWA_HD, 1)], axis=1).astype(BF16)
    sv = skv[:, D_KV:]
    osv[0] = jnp.concatenate([sv, pltpu.roll(sv, SWA_HD, 1)], axis=1).astype(BF16)


def _prep(pqkv, psq, pskv, pab, cos_t, sin_t, conv_w, gate_p, qg, kg, bd, ctx):
    n_batch, t, cq = pqkv.shape
    tm = TOK_TILE
    hb = tm // SUBLANES
    nh = t // SUBLANES
    body = functools.partial(_prep_body, tm=tm, ctx=ctx, t=t)
    tok = lambda w: pl.BlockSpec((1, tm, w), lambda b, i: (b, i, 0))
    full = lambda a: pl.BlockSpec(a.shape, lambda b, i: (0,) * a.ndim)
    return pl.pallas_call(
        body,
        grid=(n_batch, t // tm),
        in_specs=[tok(cq),
                  pl.BlockSpec((1, SUBLANES, cq), lambda b, i: (b, jnp.maximum(i * hb - 1, 0), 0)),
                  pl.BlockSpec((1, SUBLANES, cq), lambda b, i: (b, jnp.minimum((i + 1) * hb, nh - 1), 0)),
                  tok(D_SWA), tok(2 * D_KV), tok(LANES),
                  pl.BlockSpec((tm, LANES), lambda b, i: (i, 0)),
                  pl.BlockSpec((tm, LANES), lambda b, i: (i, 0)),
                  full(conv_w), full(gate_p), full(qg), full(kg), full(bd)],
        out_specs=[tok(DN_QK), tok(DN_QK), tok(D_DN), tok(LANES), tok(D_SWA), tok(2 * D_KV), tok(2 * D_KV)],
        out_shape=[jax.ShapeDtypeStruct((n_batch, t, DN_QK), F32),
                   jax.ShapeDtypeStruct((n_batch, t, DN_QK), F32),
                   jax.ShapeDtypeStruct((n_batch, t, D_DN), F32),
                   jax.ShapeDtypeStruct((n_batch, t, LANES), F32),
                   jax.ShapeDtypeStruct((n_batch, t, D_SWA), BF16),
                   jax.ShapeDtypeStruct((n_batch, t, 2 * D_KV), BF16),
                   jax.ShapeDtypeStruct((n_batch, t, 2 * D_KV), BF16)],
        compiler_params=_cparams(("parallel", "parallel")),
        name="prep",
    )(pqkv, pqkv, pqkv, psq, pskv, pab, cos_t, sin_t, conv_w, gate_p, qg, kg, bd)


def _dn_dir_setup(q_ref, k_ref, v_ref, g_ref, tri, row0, rev, b):
    c = DN_CHUNK
    r = pl.ds(pl.multiple_of(row0, c), c)
    gb = g_ref[b, r, :]
    hi = gb.astype(BF16)
    r1 = gb - hi.astype(F32)
    mid = r1.astype(BF16)
    lo = (r1 - mid.astype(F32)).astype(BF16)
    gc = (jnp.dot(tri, hi, preferred_element_type=F32) + jnp.dot(tri, mid, preferred_element_type=F32)
          + jnp.dot(tri, lo, preferred_element_type=F32))
    last = 0 if rev else c - 1
    gl = gc[last:last + 1, :]
    ii = lax.broadcasted_iota(jnp.int32, (c, LANES), 0)
    jj = lax.broadcasted_iota(jnp.int32, (c, LANES), 1) & (c - 1)
    return dict(r=r, q=q_ref[b, r, :], k=k_ref[b, r, :], v=v_ref[b, r, :], gb=gb, gc=gc, gct=gc.T,
                eg=jnp.exp(gc), kdf=jnp.exp(gl - gc), egl=jnp.exp(gl),
                incl=(ii <= jj) if rev else (ii >= jj), strict=(ii < jj) if rev else (ii > jj),
                eye=(ii == jj).astype(F32))


def _dn_pair(fwd_refs, bwd_refs, of_ref, ob_ref, s_ref, tri_ref, row_f, row_b, n_batch):
    c = DN_CHUNK
    dirs = []
    for b in range(n_batch):
        dirs += [_dn_dir_setup(*fwd_refs, tri_ref[0], row_f, False, b),
                 _dn_dir_setup(*bwd_refs, tri_ref[1], row_b, True, b)]
    npair = DN_HEADS // 2
    chains = [(d, p) for d in range(2 * n_batch) for p in range(npair)]
    dot = functools.partial(jnp.dot, preferred_element_type=F32)
    lane = lax.broadcasted_iota(jnp.int32, (1, LANES), 1)
    low = lane < DN_DK
    same_head = ((lax.broadcasted_iota(jnp.int32, (LANES, LANES), 0) < c)
                 == (lax.broadcasted_iota(jnp.int32, (LANES, LANES), 1) < DN_DK))

    def bdiag(m):
        return jnp.where(same_head, jnp.concatenate([m, m], axis=0), 0.0).astype(BF16)

    def tile(name, d, p):
        return dirs[d][name][:, p * LANES:(p + 1) * LANES]

    def col(name, d, p, off=0):
        ln = off + (d % 2) * DN_HEADS + 2 * p
        a = dirs[d][name]
        return jnp.where(low, a[:, ln:ln + 1], a[:, ln + 1:ln + 2])

    kp = [tile("k", d, p) for d, p in chains]
    qp = [tile("q", d, p) for d, p in chains]
    vp = [tile("v", d, p) for d, p in chains]
    beta = [col("gb", d, p, 2 * DN_HEADS) for d, p in chains]
    egc = [col("eg", d, p) for d, p in chains]
    decay = []
    for d, p in chains:
        ln = (d % 2) * DN_HEADS + 2 * p
        gct = dirs[d]["gct"]
        grow = jnp.concatenate([gct[ln:ln + 1, :], gct[ln + 1:ln + 2, :]], axis=1)
        incl = dirs[d]["incl"]
        diff = col("gc", d, p) - grow
        decay.append(jnp.where(incl, jnp.exp(jnp.where(incl, diff, 0.0)), 0.0))
    kbp = [k * b for k, b in zip(kp, beta)]
    kq = [lax.dot_general(jnp.concatenate([kb, q], axis=0).astype(BF16), bdiag(k),
                          (((1,), (1,)), ((), ())), preferred_element_type=F32)
          for kb, q, k in zip(kbp, qp, kp)]
    lmat = [jnp.where(dirs[d]["strict"], a[:c] * dc, 0.0) for (d, p), a, dc in zip(chains, kq, decay)]
    aqk = [a[c:] * dc for a, dc in zip(kq, decay)]
    ii = lax.broadcasted_iota(jnp.int32, (c, LANES), 0)
    jj = lax.broadcasted_iota(jnp.int32, (c, LANES), 1) & (c - 1)
    eye = dirs[0]["eye"]
    base = DN_INV_BASE_LOG2
    assert base == 3
    ld = [jnp.where((ii >> base) == (jj >> base), m, 0.0) for m in lmat]
    pw = [dot(m.astype(BF16), bdiag(m)) for m in ld]
    r = [dot(jnp.concatenate([eye - m, pi], axis=0).astype(BF16), bdiag(pi)) for m, pi in zip(ld, pw)]
    x = [eye - m + ri[:c] for m, ri in zip(ld, r)]
    minv = [xi + dot(xi.astype(BF16), bdiag(ri[c:])) for xi, ri in zip(x, r)]
    for lb in range(base, 6):
        joined = ((ii >> (lb + 1)) == (jj >> (lb + 1))) & (((ii >> lb) & 1) != ((jj >> lb) & 1))
        mc = [dot(mi.astype(BF16), bdiag(jnp.where(joined, m, 0.0))) for mi, m in zip(minv, lmat)]
        minv = [mi - dot(mci.astype(BF16), bdiag(mi)) for mi, mci in zip(minv, mc)]
    tinv = [m.astype(BF16) for m in minv]
    u = [dot(ti, bdiag(v * b)) for ti, v, b in zip(tinv, vp, beta)]
    w = [dot(ti, bdiag(kb * e)) for ti, kb, e in zip(tinv, kbp, egc)]
    kdt = [(k * col("kdf", d, p)).T.astype(BF16) for (d, p), k in zip(chains, kp)]
    s = [s_ref[d, p] for d, p in chains]
    r2 = [dot(jnp.concatenate([wi, q * e], axis=0).astype(BF16), bdiag(si))
          for wi, q, e, si in zip(w, qp, egc, s)]
    vnew = [ui - ri[:c] for ui, ri in zip(u, r2)]
    r3 = [dot(a.astype(BF16), bdiag(vn)) for a, vn in zip(aqk, vnew)]
    sup = [dot(kt, vn.astype(BF16)) for kt, vn in zip(kdt, vnew)]
    for n, (d, p) in enumerate(chains):
        s_ref[d, p] = s[n] * col("egl", d, p) + jnp.where(low, sup[n][:c], sup[n][c:])
    outs = [ri[c:] + r3i for ri, r3i in zip(r2, r3)]
    for b in range(n_batch):
        lo = 2 * b * npair
        of_ref[b, dirs[2 * b]["r"], :] = jnp.concatenate(outs[lo:lo + npair], axis=1)
        ob_ref[b, dirs[2 * b + 1]["r"], :] = jnp.concatenate(outs[lo + npair:lo + 2 * npair], axis=1)


def _dn_body(qf, kf, vf, gf, qb, kb, vb, gbk, tri_ref, of_ref, ob_ref, s_ref, *, nchunk, n_batch):
    i = pl.program_id(0)

    @pl.when(i == 0)
    def _():
        s_ref[...] = jnp.zeros_like(s_ref)

    def step(cidx, carry):
        _dn_pair((qf, kf, vf, gf), (qb, kb, vb, gbk), of_ref, ob_ref, s_ref, tri_ref,
                 cidx * DN_CHUNK, (nchunk - 1 - cidx) * DN_CHUNK, n_batch)
        return carry

    lax.fori_loop(0, nchunk, step, 0)


def _deltanet(dq, dk, dv, gb, tri, ctx):
    n_batch, t, _ = dq.shape
    tb = TOK_TILE
    nblk = t // tb
    nctx = ctx // tb

    def bmap(i):
        return jnp.where(i < nctx, nctx - 1 - i, nblk - 1 - (i - nctx))

    fwd = lambda w: pl.BlockSpec((n_batch, tb, w), lambda i: (0, i, 0))
    bwd = lambda w: pl.BlockSpec((n_batch, tb, w), lambda i: (0, bmap(i), 0))
    body = functools.partial(_dn_body, nchunk=tb // DN_CHUNK, n_batch=n_batch)
    return pl.pallas_call(
        body,
        grid=(nblk,),
        in_specs=[fwd(DN_QK), fwd(DN_QK), fwd(D_DN), fwd(LANES),
                  bwd(DN_QK), bwd(DN_QK), bwd(D_DN), bwd(LANES),
                  pl.BlockSpec(tri.shape, lambda i: (0, 0, 0))],
        out_specs=[fwd(D_DN), bwd(D_DN)],
        out_shape=[jax.ShapeDtypeStruct((n_batch, t, D_DN), F32)] * 2,
        scratch_shapes=[pltpu.VMEM((2 * n_batch, DN_HEADS // 2, DN_DK, 2 * DN_DV), F32)],
        compiler_params=_cparams(("arbitrary",)),
        name="deltanet",
    )(dq, dk, dv, gb, dq, dk, dv, gb, tri)


def _swa_body(sink_ref, q_ref, kp, kc, kn, vp, vc, vn, kx, vx, o_ref, *, ctx, t):
    i = pl.program_id(1)
    bl = SWA_BLOCK
    nband = 3 * bl
    ncol = nband + ctx
    q = q_ref[0]
    row2 = lax.broadcasted_iota(jnp.int32, (2 * bl, 1), 0)
    qpos = i * bl + (row2 & (bl - 1))
    col = lax.broadcasted_iota(jnp.int32, (1, ncol), 1)
    kpos = (i - 1) * bl + col
    band_ok = ((jnp.abs(kpos - qpos) <= SWA_WINDOW) & (kpos >= ctx) & (kpos < t) & (qpos >= ctx)
               & (col < nband))
    mask = band_ok | (col >= nband)
    k2 = jnp.concatenate([kp[0], kc[0], kn[0], kx[0]], axis=0)
    v2 = jnp.concatenate([vp[0], vc[0], vn[0], vx[0]], axis=0)
    low = lax.broadcasted_iota(jnp.int32, (ncol, LANES), 1) < SWA_HD
    zero = jnp.zeros((ncol, LANES), BF16)
    pick = lambda a: [jnp.where(low, a[:, :LANES], zero), jnp.where(low, zero, a[:, LANES:]),
                      jnp.where(low, a[:, LANES:], zero), jnp.where(low, zero, a[:, :LANES])]
    kvar = pick(k2)
    vvar = pick(v2)
    pairs = D_SWA // LANES
    qg = [jnp.concatenate([q[:, (2 * g) * LANES:(2 * g + 1) * LANES],
                           q[:, (2 * g + 1) * LANES:(2 * g + 2) * LANES]], axis=0) for g in range(pairs // 2)]
    cases = [(g, par) for g in range(pairs // 2) for par in range(2)]
    scores = [lax.dot_general(qg[g], kvar[2 * g + par], (((1,), (1,)), ((), ())), preferred_element_type=F32)
              for g, par in cases]
    probs, rden = [], []
    for (g, par), s in zip(cases, scores):
        h_top = 4 * g + par
        sink = jnp.where(row2 < bl, sink_ref[h_top], sink_ref[h_top + 2])
        s = jnp.where(mask, s, NEG_INF)
        m = jnp.maximum(jnp.max(s, axis=-1, keepdims=True), sink)
        p = jnp.exp(s - m)
        rden.append(1.0 / (jnp.sum(p, axis=-1, keepdims=True) + jnp.exp(sink - m)))
        probs.append(p.astype(BF16))
    pv = [jnp.dot(p, vvar[2 * g + par], preferred_element_type=F32) * r
          for (g, par), p, r in zip(cases, probs, rden)]
    outs = []
    for g in range(pairs // 2):
        og = pv[2 * g] + pv[2 * g + 1]
        outs += [og[:bl], og[bl:]]
    o_ref[0] = jnp.concatenate(outs, axis=1).astype(BF16)


def _swa(sq, sk, sv, sinks, ctx):
    n_batch, t, _ = sq.shape
    bl = SWA_BLOCK
    nb = t // bl
    body = functools.partial(_swa_body, ctx=ctx, t=t)
    kvspec = lambda f: pl.BlockSpec((1, bl, 2 * D_KV), f)
    prev = lambda b, i: (b, jnp.maximum(i - 1, 0), 0)
    cur = lambda b, i: (b, i, 0)
    nxt = lambda b, i: (b, jnp.minimum(i + 1, nb - 1), 0)
    cspec = pl.BlockSpec((1, ctx, 2 * D_KV), lambda b, i: (b, 0, 0))
    return pl.pallas_call(
        body,
        grid=(n_batch, nb),
        in_specs=[pl.BlockSpec(memory_space=pltpu.SMEM),
                  pl.BlockSpec((1, bl, D_SWA), cur),
                  kvspec(prev), kvspec(cur), kvspec(nxt),
                  kvspec(prev), kvspec(cur), kvspec(nxt),
                  cspec, cspec],
        out_specs=pl.BlockSpec((1, bl, D_SWA), cur),
        out_shape=jax.ShapeDtypeStruct((n_batch, t, D_SWA), BF16),
        compiler_params=_cparams(("parallel", "parallel")),
        name="swa",
    )(sinks, sq, sk, sk, sk, sv, sv, sv, sk, sv)


def _post_body(of_ref, ob_ref, z_ref, sw_ref, x_ref, mod_ref, og_ref, wo_ref, g2_ref, rw_ref, rb_ref,
               bd_ref, tri_ref, xo_ref, h_ref, route_ref, cnt_ref, *, tm, ctx, n_batch, d):
    b = pl.program_id(0)
    i = pl.program_id(1)
    rows = i * tm + lax.broadcasted_iota(jnp.int32, (tm, 1), 0)
    is_ctx = rows < ctx
    o = of_ref[0] + ob_ref[0]
    ms = _head_sum(o * o, bd_ref[...]) * (1.0 / DN_DV)
    z = z_ref[0]
    dn = o * lax.rsqrt(ms + NORM_EPS) * og_ref[...] * (z * _sigmoid(z))
    mix = jnp.concatenate([dn.astype(BF16), sw_ref[0]], axis=1)
    proj = jnp.dot(mix, wo_ref[...], preferred_element_type=F32)
    gt1 = _mod_chunk(mod_ref, b, n_batch, is_ctx, 2, d)
    x = x_ref[0] + gt1 * proj
    xo_ref[0] = x

    ms2 = jnp.mean(x * x, axis=-1, keepdims=True)
    y = x * lax.rsqrt(ms2 + NORM_EPS) * g2_ref[...]
    sh2 = _mod_chunk(mod_ref, b, n_batch, is_ctx, 3, d)
    sc2 = _mod_chunk(mod_ref, b, n_batch, is_ctx, 4, d)
    h = y * (1.0 + sc2) + sh2
    h_ref[0] = h

    h_hi = h.astype(BF16)
    h_lo = (h - h_hi.astype(F32)).astype(BF16)
    logits = (jnp.dot(h_hi, rw_ref[0], preferred_element_type=F32)
              + jnp.dot(h_lo, rw_ref[0], preferred_element_type=F32)
              + jnp.dot(h_hi, rw_ref[1], preferred_element_type=F32)
              + jnp.dot(h_lo, rw_ref[1], preferred_element_type=F32)) + rb_ref[...]
    lane = lax.broadcasted_iota(jnp.int32, (1, LANES), 1)
    vals, idxs = [], []
    l = logits
    for _ in range(TOP_K):
        m = jnp.max(l, axis=-1, keepdims=True)
        idx = jnp.min(jnp.where(l == m, lane, LANES), axis=-1, keepdims=True)
        vals.append(m)
        idxs.append(idx)
        l = jnp.where(lane == idx, -3e38, l)
    es = [jnp.exp(v - vals[0]) for v in vals]
    den = es[0] + es[1] + es[2] + es[3]
    gates = [e / den for e in es]

    @pl.when((b == 0) & (i == 0))
    def _():
        cnt_ref[...] = jnp.zeros_like(cnt_ref)

    onehot = jnp.zeros((tm, LANES), F32)
    for idx in idxs:
        onehot = onehot + (lane == idx).astype(F32)
    run = cnt_ref[0:1, :]
    cum = jnp.dot(tri_ref[...], onehot.astype(BF16), preferred_element_type=F32) + run
    ranks = [jnp.sum(jnp.where(lane == idx, cum, 0.0), axis=-1, keepdims=True) for idx in idxs]
    cnt_ref[...] = jnp.broadcast_to(run + jnp.sum(onehot, axis=0, keepdims=True), cnt_ref.shape)
    route = jnp.zeros((tm, LANES), F32)
    for k in range(TOP_K):
        route = jnp.where(lane == k, idxs[k].astype(F32), route)
        route = jnp.where(lane == TOP_K + k, gates[k], route)
        route = jnp.where(lane == 2 * TOP_K + k, ranks[k], route)
    route_ref[0] = route


def _post(o_f, o_b, pz, o_sw, xu, mod, og, w_out, g2, rw, rb, bd, tri, ctx):
    n_batch, t, d = xu.shape
    tm = TOK_TILE
    body = functools.partial(_post_body, tm=tm, ctx=ctx, n_batch=n_batch, d=d)
    tok = lambda w: pl.BlockSpec((1, tm, w), lambda b, i: (b, i, 0))
    full = lambda a: pl.BlockSpec(a.shape, lambda b, i: (0,) * a.ndim)
    return pl.pallas_call(
        body,
        grid=(n_batch, t // tm),
        in_specs=[tok(D_DN), tok(D_DN), tok(D_DN), tok(D_SWA), tok(d),
                  full(mod), full(og), full(w_out), full(g2), full(rw), full(rb), full(bd), full(tri)],
        out_specs=[tok(d), tok(d), tok(LANES), pl.BlockSpec((SUBLANES, LANES), lambda b, i: (0, 0))],
        out_shape=[jax.ShapeDtypeStruct((n_batch, t, d), F32),
                   jax.ShapeDtypeStruct((n_batch, t, d), F32),
                   jax.ShapeDtypeStruct((n_batch, t, LANES), F32),
                   jax.ShapeDtypeStruct((SUBLANES, LANES), F32)],
        compiler_params=_cparams(("arbitrary", "arbitrary")),
        name="post_attn",
    )(o_f, o_b, pz, o_sw, xu, mod, og, w_out, g2, rw, rb, bd, tri)


def _dispatch_body(dest_ref, h_ref, xs_ref, sem, *, td):
    def issue(n, carry):
        for u in range(ROW_DMA_UNROLL):
            tk = n * ROW_DMA_UNROLL + u
            for k in range(TOP_K):
                dst = dest_ref[0, 0, tk * TOP_K + k]
                pltpu.make_async_copy(h_ref.at[pl.ds(tk, 1)], xs_ref.at[pl.ds(dst, 1)], sem).start(priority=k % 2)
        return carry

    lax.fori_loop(0, td // ROW_DMA_UNROLL, issue, 0)

    def drain(n, carry):
        for _ in range(ROW_DMA_UNROLL * TOP_K):
            pltpu.make_async_copy(h_ref.at[pl.ds(0, 1)], xs_ref.at[pl.ds(0, 1)], sem).wait()
        return carry

    lax.fori_loop(0, td // ROW_DMA_UNROLL, drain, 0)


def _dispatch(h_flat, dest, n_slots):
    ntok, d = h_flat.shape
    td = TOK_TILE
    body = functools.partial(_dispatch_body, td=td)
    return pl.pallas_call(
        body,
        grid=(ntok // td,),
        in_specs=[pl.BlockSpec((1, 1, td * TOP_K), lambda i: (i, 0, 0), memory_space=pltpu.SMEM),
                  pl.BlockSpec((td, d), lambda i: (i, 0))],
        out_specs=pl.BlockSpec(memory_space=pl.ANY),
        out_shape=jax.ShapeDtypeStruct((n_slots, d), F32),
        scratch_shapes=[pltpu.SemaphoreType.DMA(())],
        compiler_params=_cparams(("arbitrary",)),
        name="moe_dispatch",
    )(dest.reshape(ntok // td, 1, td * TOP_K), h_flat)


def _cast_rows(src_ref, dst_ref, rows_per_step):
    def step(n, carry):
        r = pl.ds(pl.multiple_of(n * rows_per_step, rows_per_step), rows_per_step)
        dst_ref[r, :] = src_ref[0, 0, r, :].astype(BF16)
        return carry

    lax.fori_loop(0, src_ref.shape[2] // rows_per_step, step, 0)


def _moe_body(te_ref, nv_ref, nu_ref, xs_ref, wgu_ref, bgu_ref, wdn_ref, bdn_ref, ys_ref, wgu_bf, wdn_bf, *, tm, de):
    j = pl.program_id(0)
    used = j < nu_ref[0]
    prev = te_ref[jnp.maximum(j - 1, 0)]

    @pl.when(used & ((j == 0) | (te_ref[j] != prev)))
    def _():
        _cast_rows(wgu_ref, wgu_bf, LANES)
        _cast_rows(wdn_ref, wdn_bf, LANES)

    @pl.when(used)
    def _():
        rows = lax.broadcasted_iota(jnp.int32, (tm, 1), 0)
        x = jnp.where(rows < nv_ref[j], xs_ref[...], 0.0).astype(BF16)
        gu = jnp.dot(x, wgu_bf[...], preferred_element_type=F32) + bgu_ref[0, 0]
        gate = jnp.minimum(gu[:, :de], SWIGLU_LIMIT)
        up = jnp.clip(gu[:, de:], -SWIGLU_LIMIT, SWIGLU_LIMIT)
        act = (up + 1.0) * gate * _sigmoid(SWIGLU_ALPHA * gate)
        ys_ref[...] = jnp.dot(act.astype(BF16), wdn_bf[...], preferred_element_type=F32) + bdn_ref[0, 0]

    @pl.when(jnp.logical_not(used))
    def _():
        ys_ref[...] = jnp.zeros_like(ys_ref)


def _moe(xs, tile_expert, tile_valid, n_used, wgu, bgu, wdn, bdn, layer):
    n_slots, d = xs.shape
    tm = MOE_TILE
    _, n_exp, _, de2 = wgu.shape
    de = de2 // 2
    body = functools.partial(_moe_body, tm=tm, de=de)
    wmap = lambda j, te, nv, nu: (layer, te[j], 0, 0)
    grid_spec = pltpu.PrefetchScalarGridSpec(
        num_scalar_prefetch=3,
        grid=(n_slots // tm,),
        in_specs=[pl.BlockSpec((tm, d), lambda j, te, nv, nu: (j, 0)),
                  pl.BlockSpec((1, 1, d, de2), wmap),
                  pl.BlockSpec((1, 1, 1, de2), wmap),
                  pl.BlockSpec((1, 1, de, d), wmap),
                  pl.BlockSpec((1, 1, 1, d), wmap)],
        out_specs=pl.BlockSpec((tm, d), lambda j, te, nv, nu: (j, 0)),
        scratch_shapes=[pltpu.VMEM((d, de2), BF16), pltpu.VMEM((de, d), BF16)],
    )
    depth = wgu.shape[0]
    return pl.pallas_call(
        body,
        grid_spec=grid_spec,
        out_shape=jax.ShapeDtypeStruct((n_slots, d), F32),
        compiler_params=_cparams(("arbitrary",), MOE_VMEM_LIMIT),
        name="moe_experts",
    )(tile_expert, tile_valid, n_used, xs, wgu, bgu.reshape(depth, n_exp, 1, de2), wdn,
      bdn.reshape(depth, n_exp, 1, d))


def _combine_body(dest_ref, ys_ref, x_ref, route_ref, mod_ref, o_ref, buf, sem, *, tc, ctx, n_batch, d):
    b = pl.program_id(0)
    i = pl.program_id(1)

    def issue(n, carry):
        for u in range(ROW_DMA_UNROLL):
            tk = n * ROW_DMA_UNROLL + u
            for k in range(TOP_K):
                src = dest_ref[0, 0, tk * TOP_K + k]
                pltpu.make_async_copy(ys_ref.at[pl.ds(src, 1)], buf.at[k, pl.ds(tk, 1)], sem).start(priority=k % 2)
        return carry

    lax.fori_loop(0, tc // ROW_DMA_UNROLL, issue, 0)

    def drain(n, carry):
        for _ in range(ROW_DMA_UNROLL * TOP_K):
            pltpu.make_async_copy(ys_ref.at[pl.ds(0, 1)], buf.at[0, pl.ds(0, 1)], sem).wait()
        return carry

    lax.fori_loop(0, tc // ROW_DMA_UNROLL, drain, 0)

    route = route_ref[0]
    y = None
    for k in range(TOP_K):
        term = route[:, TOP_K + k:TOP_K + k + 1] * buf[k]
        y = term if y is None else y + term
    rows = i * tc + lax.broadcasted_iota(jnp.int32, (tc, 1), 0)
    gt2 = _mod_chunk(mod_ref, b, n_batch, rows < ctx, 5, d)
    o_ref[0] = x_ref[0] + gt2 * y


def _combine(dest, ys, xu, route, mod, ctx):
    n_batch, t, d = xu.shape
    tc = TOK_TILE
    nt = t // tc
    body = functools.partial(_combine_body, tc=tc, ctx=ctx, n_batch=n_batch, d=d)
    return pl.pallas_call(
        body,
        grid=(n_batch, nt),
        in_specs=[pl.BlockSpec((1, 1, tc * TOP_K), lambda b, i: (b * nt + i, 0, 0), memory_space=pltpu.SMEM),
                  pl.BlockSpec(memory_space=pl.ANY),
                  pl.BlockSpec((1, tc, d), lambda b, i: (b, i, 0)),
                  pl.BlockSpec((1, tc, LANES), lambda b, i: (b, i, 0)),
                  pl.BlockSpec(mod.shape, lambda b, i: (0, 0))],
        out_specs=pl.BlockSpec((1, tc, d), lambda b, i: (b, i, 0)),
        out_shape=jax.ShapeDtypeStruct((n_batch, t, d), F32),
        scratch_shapes=[pltpu.VMEM((TOP_K, tc, d), F32), pltpu.SemaphoreType.DMA(())],
        compiler_params=_cparams(("arbitrary", "arbitrary")),
        name="moe_combine",
    )(dest.reshape(n_batch * nt, 1, tc * TOP_K), ys, xu, route, mod)


def _rope_tables(seq, ctx):
    rows = seq // GRID_W
    row = jnp.repeat(jnp.arange(rows, dtype=F32), GRID_W)
    col = jnp.tile(jnp.arange(GRID_W, dtype=F32), rows)
    inv_freq = jnp.power(ROPE_THETA, -jnp.arange(ROPE_FREQS, dtype=F32) / ROPE_FREQS)
    ang_r = row[:, None] * inv_freq
    ang_c = col[:, None] * inv_freq
    cos = jnp.concatenate([jnp.cos(ang_r), jnp.cos(ang_r), jnp.cos(ang_c), jnp.cos(ang_c)], axis=1)
    sin = jnp.concatenate([-jnp.sin(ang_r), jnp.sin(ang_r), -jnp.sin(ang_c), jnp.sin(ang_c)], axis=1)
    cos = jnp.concatenate([jnp.ones((ctx, SWA_HD), F32), cos], axis=0)
    sin = jnp.concatenate([jnp.zeros((ctx, SWA_HD), F32), sin], axis=0)
    reps = LANES // SWA_HD
    return jnp.tile(cos, (1, reps)), jnp.tile(sin, (1, reps))


def _pad_w_in(w_in):
    sizes = (DN_QK, DN_QK, D_DN, D_DN, 2 * DN_HEADS, 2 * DN_HEADS, D_SWA, D_KV, D_KV)
    offs = np.concatenate([[0], np.cumsum(sizes)])
    part = lambda n: w_in[:, offs[n]:offs[n + 1]]
    pad = jnp.zeros((w_in.shape[0], LANES - 4 * DN_HEADS), w_in.dtype)
    return jnp.concatenate([part(0), part(1), part(2), part(3), part(6), part(7), part(8), part(4), part(5), pad],
                           axis=1).astype(BF16)


def _lane_row(v, fill=0.0):
    out = jnp.full((1, LANES), fill, F32)
    return out.at[0, :v.shape[0]].set(v.astype(F32))


def kernel(x, c, ctx, c_ctx, ada_w, ada_b, norm1_g, w_in, dn_conv_w, dn_a_log, dn_dt_bias, dn_out_g, q_norm_g,
           k_norm_g, sinks, w_out, norm2_g, router_w, router_b, w_gate_up, b_gate_up, w_down, b_down):
    n_batch, seq, d = x.shape
    n_ctx = ctx.shape[1]
    depth = ada_w.shape[0]
    t = n_ctx + seq
    ntok = n_batch * t
    assert n_batch < SUBLANES and n_ctx % TOK_TILE == 0 and seq % TOK_TILE == 0

    xu = jnp.concatenate([ctx, x], axis=1)
    cvec = jnp.zeros((SUBLANES, d), F32).at[:n_batch].set(c).at[n_batch].set(c_ctx)
    cos_t, sin_t = _rope_tables(seq, n_ctx)
    hid = np.arange(DN_QK) // DN_DK
    bd = jnp.asarray(hid[:, None] == hid[None, :], BF16)
    cidx = np.arange(DN_CHUNK)
    tri_dn = jnp.asarray(np.stack([cidx[:, None] >= cidx[None, :], cidx[:, None] <= cidx[None, :]]), BF16)
    ridx = np.arange(TOK_TILE)
    tri_rank = jnp.asarray(ridx[:, None] > ridx[None, :], BF16)

    n_slots = ntok * TOP_K + N_EXPERTS * MOE_TILE
    n_tiles = n_slots // MOE_TILE
    eid = jnp.arange(N_EXPERTS, dtype=jnp.int32)

    for l in range(depth):
        mod = _ada(cvec, ada_w, ada_b, l)
        pqkv, pz, psq, pskv, pab = _inproj(xu, mod, norm1_g[l], _pad_w_in(w_in[l]), n_ctx)
        conv_w = jnp.zeros((SUBLANES, dn_conv_w.shape[2]), F32).at[:DN_CONV].set(dn_conv_w[l])
        gate_p = jnp.concatenate([_lane_row(dn_a_log[l].reshape(-1)), _lane_row(dn_dt_bias[l].reshape(-1)),
                                  jnp.zeros((SUBLANES - 2, LANES), F32)], axis=0)
        qg = jnp.tile(q_norm_g[l], SWA_Q_HEADS).reshape(1, D_SWA)
        kg = jnp.tile(k_norm_g[l], SWA_KV_HEADS).reshape(1, D_KV)
        dq, dk, dv, gb, sq, sk, sv = _prep(pqkv, psq, pskv, pab, cos_t, sin_t, conv_w, gate_p, qg, kg, bd, n_ctx)
        o_f, o_b = _deltanet(dq, dk, dv, gb, tri_dn, n_ctx)
        o_sw = _swa(sq, sk, sv, sinks[l], n_ctx)
        og = jnp.tile(dn_out_g[l], DN_HEADS).reshape(1, D_DN)
        rw32 = jnp.zeros((d, LANES), F32).at[:, :N_EXPERTS].set(router_w[l])
        rw_hi = rw32.astype(BF16)
        rw = jnp.stack([rw_hi, (rw32 - rw_hi.astype(F32)).astype(BF16)])
        rb = _lane_row(router_b[l], NEG_INF)
        xu, h, route, cnt = _post(o_f, o_b, pz, o_sw, xu, mod, og, w_out[l].astype(BF16),
                                  norm2_g[l].reshape(1, d), rw, rb, bd, tri_rank, n_ctx)

        route_flat = route.reshape(ntok, LANES)
        idx = route_flat[:, :TOP_K].astype(jnp.int32)
        rank = route_flat[:, 2 * TOP_K:3 * TOP_K].astype(jnp.int32)
        counts = cnt[0, :N_EXPERTS].astype(jnp.int32)
        padded = (counts + MOE_TILE - 1) // MOE_TILE * MOE_TILE
        pad_end = jnp.cumsum(padded)
        pad_start = pad_end - padded
        dest = jnp.sum(jnp.where(idx[:, :, None] == eid[None, None, :], pad_start[None, None, :], 0), axis=-1) + rank
        tile_lo = jnp.arange(n_tiles, dtype=jnp.int32) * MOE_TILE
        tile_expert = jnp.minimum(jnp.sum(pad_end[None, :] <= tile_lo[:, None], axis=1), N_EXPERTS - 1).astype(jnp.int32)
        tile_valid = jnp.clip((pad_start + counts)[tile_expert] - tile_lo, 0, MOE_TILE).astype(jnp.int32)
        n_used = (pad_end[-1:] // MOE_TILE).astype(jnp.int32)

        xs = _dispatch(h.reshape(ntok, d), dest, n_slots)
        ys = _moe(xs, tile_expert, tile_valid, n_used, w_gate_up, b_gate_up, w_down, b_down, l)
        xu = _combine(dest, ys, xu, route, mod, n_ctx)

    return xu[:, n_ctx:, :]
```

```python
import functools
import math

import jax
import jax.numpy as jnp
import numpy as np
from jax import lax
from jax.experimental import pallas as pl
from jax.experimental.pallas import tpu as pltpu

F32 = jnp.float32
BF16 = jnp.bfloat16
HIGHEST = lax.Precision.HIGHEST

GRID_W = 64
DN_HEADS = 8
DN_DK = 64
DN_DV = 64
DN_CONV = 5
DN_CHUNK = 64
SWA_Q_HEADS = 8
SWA_KV_HEADS = 2
SWA_HD = 64
SWA_WINDOW = 128
SWA_BLOCK = 128
ROPE_THETA = 10000.0
ROPE_FREQS = SWA_HD // 4
DN_QK = DN_HEADS * DN_DK
D_DN = DN_HEADS * DN_DV
D_SWA = SWA_Q_HEADS * SWA_HD
D_KV = SWA_KV_HEADS * SWA_HD
N_EXPERTS = 32
TOP_K = 4
SWIGLU_LIMIT = 7.0
SWIGLU_ALPHA = 1.702
NORM_EPS = 1e-6
NEG_INF = -1e30

LANES = 128
SUBLANES = 8
TOK_TILE = 256
MOE_TILE = 256
DN_INV_BASE_LOG2 = 3
ROW_DMA_UNROLL = 4
VMEM_LIMIT = 48 * 1024 * 1024
MOE_VMEM_LIMIT = 52 * 1024 * 1024


def _sigmoid(x):
    return 1.0 / (1.0 + jnp.exp(-x))


def _cparams(sem, vmem=None):
    return pltpu.CompilerParams(dimension_semantics=sem, vmem_limit_bytes=vmem or VMEM_LIMIT)


def _head_sum(x2, bd):
    hi = x2.astype(BF16)
    lo = (x2 - hi.astype(F32)).astype(BF16)
    return (jnp.dot(hi, bd, preferred_element_type=F32) + jnp.dot(lo, bd, preferred_element_type=F32))


def _mod_chunk(mod_ref, b, n_batch, is_ctx, k, d):
    mb = mod_ref[pl.ds(b, 1), k * d:(k + 1) * d]
    mc = mod_ref[n_batch:n_batch + 1, k * d:(k + 1) * d]
    return jnp.where(is_ctx, mc, mb)


def _ada_body(c_ref, w_ref, b_ref, o_ref):
    c = c_ref[...]
    s = c * _sigmoid(c)
    o_ref[...] = jnp.dot(s, w_ref[0], precision=HIGHEST, preferred_element_type=F32) + b_ref[0]


def _ada(cvec, ada_w, ada_b, layer):
    depth, d, n = ada_w.shape
    tn = n // 4
    return pl.pallas_call(
        _ada_body,
        grid=(n // tn,),
        in_specs=[pl.BlockSpec((SUBLANES, d), lambda j: (0, 0)),
                  pl.BlockSpec((1, d, tn), lambda j: (layer, 0, j)),
                  pl.BlockSpec((1, 1, tn), lambda j: (layer, 0, j))],
        out_specs=pl.BlockSpec((SUBLANES, tn), lambda j: (0, j)),
        out_shape=jax.ShapeDtypeStruct((SUBLANES, n), F32),
        compiler_params=_cparams(("arbitrary",)),
        name="ada_mod",
    )(cvec, ada_w, ada_b.reshape(depth, 1, n))


def _swap16(x):
    lane = lax.broadcasted_iota(jnp.int32, (1, LANES), 1)
    even = ((lane >> 4) & 1) == 0
    nxt = pltpu.roll(x, LANES - 16, 1)
    prv = pltpu.roll(x, 16, 1)
    return jnp.where(even, nxt, prv)


def _rope(x, cos, sin):
    outs = []
    for c in range(x.shape[1] // LANES):
        xc = x[:, c * LANES:(c + 1) * LANES]
        outs.append(xc * cos + _swap16(xc) * sin)
    return outs[0] if len(outs) == 1 else jnp.concatenate(outs, axis=1)


def _inprep_body(x_ref, xp_ref, xn_ref, mod_ref, g_ref, w_ref, cos_ref, sin_ref, cw_ref, gp_ref,
                 qg_ref, kg_ref, bd_ref,
                 oq, ok, ov, ogb, osq, osk, osv, oz, *, tm, ctx, t, n_batch, d):
    b = pl.program_id(0)
    i = pl.program_id(1)
    lo = i * tm
    hi = lo + tm
    xe = jnp.concatenate([xp_ref[0], x_ref[0], xn_ref[0]], axis=0)
    ms = jnp.mean(xe * xe, axis=-1, keepdims=True)
    y = xe * lax.rsqrt(ms + NORM_EPS) * g_ref[...]
    is_ctx = lo < ctx
    sh = _mod_chunk(mod_ref, b, n_batch, is_ctx, 0, d)
    sc = _mod_chunk(mod_ref, b, n_batch, is_ctx, 1, d)
    h = (y * (1.0 + sc) + sh).astype(BF16)
    p = jnp.dot(h, w_ref[...], preferred_element_type=F32)
    c0 = 2 * DN_QK + D_DN
    c1 = c0 + D_DN
    c2 = c1 + D_SWA
    c3 = c2 + 2 * D_KV
    own = slice(SUBLANES, SUBLANES + tm)
    oz[0] = p[own, c0:c1]
    prv = jnp.where((lo == 0) | (lo == ctx), 0.0, p[:SUBLANES, :c0])
    nxt = jnp.where((hi == ctx) | (hi == t), 0.0, p[SUBLANES + tm:, :c0])
    xe = jnp.concatenate([prv, p[own, :c0], nxt], axis=0)
    pad = DN_CONV // 2
    acc = None
    for j in range(DN_CONV):
        term = xe[SUBLANES - pad + j:SUBLANES - pad + j + tm] * cw_ref[j:j + 1, :]
        acc = term if acc is None else acc + term
    y = acc * _sigmoid(acc)
    bd = bd_ref[...]
    q = y[:, :DN_QK]
    k = y[:, DN_QK:2 * DN_QK]
    oq[0] = q * lax.rsqrt(_head_sum(q * q, bd) + NORM_EPS) * (DN_DK ** -0.5)
    ok[0] = k * lax.rsqrt(_head_sum(k * k, bd) + NORM_EPS)
    ov[0] = y[:, 2 * DN_QK:]

    ab = p[own, c3:c3 + LANES]
    lane = lax.broadcasted_iota(jnp.int32, (1, LANES), 1)
    z = ab + gp_ref[1:2, :]
    softplus = jnp.maximum(z, 0.0) + jnp.log(1.0 + jnp.exp(-jnp.abs(z)))
    g = -jnp.exp(gp_ref[0:1, :]) * softplus
    ogb[0] = jnp.where(lane < 2 * DN_HEADS, g, _sigmoid(ab))

    cos = cos_ref[...]
    sin = sin_ref[...]
    sq = p[own, c1:c2]
    qn = sq * lax.rsqrt(_head_sum(sq * sq, bd) * (1.0 / SWA_HD) + NORM_EPS) * qg_ref[...]
    osq[0] = (_rope(qn, cos, sin) * (SWA_HD ** -0.5)).astype(BF16)
    skv = p[own, c2:c3]
    sk = skv[:, :D_KV]
    kn = sk * lax.rsqrt(_head_sum(sk * sk, bd[:D_KV, :D_KV]) * (1.0 / SWA_HD) + NORM_EPS) * kg_ref[...]
    kr = _rope(kn, cos, sin)
    osk[0] = jnp.concatenate([kr, pltpu.roll(kr, SWA_HD, 1)], axis=1).astype(BF16)
    sv = skv[:, D_KV:]
    osv[0] = jnp.concatenate([sv, pltpu.roll(sv, SWA_HD, 1)], axis=1).astype(BF16)


def _inprep(xu, mod, g1, w_pad, cos_t, sin_t, conv_w, gate_p, qg, kg, bd, ctx):
    n_batch, t, d = xu.shape
    tm = TOK_TILE
    hb = tm // SUBLANES
    nh = t // SUBLANES
    body = functools.partial(_inprep_body, tm=tm, ctx=ctx, t=t, n_batch=n_batch, d=d)
    tok = lambda w: pl.BlockSpec((1, tm, w), lambda b, i: (b, i, 0))
    full = lambda a: pl.BlockSpec(a.shape, lambda b, i: (0,) * a.ndim)
    return pl.pallas_call(
        body,
        grid=(n_batch, t // tm),
        in_specs=[tok(d),
                  pl.BlockSpec((1, SUBLANES, d), lambda b, i: (b, jnp.maximum(i * hb - 1, 0), 0)),
                  pl.BlockSpec((1, SUBLANES, d), lambda b, i: (b, jnp.minimum((i + 1) * hb, nh - 1), 0)),
                  full(mod), full(g1), full(w_pad),
                  pl.BlockSpec((tm, LANES), lambda b, i: (i, 0)),
                  pl.BlockSpec((tm, LANES), lambda b, i: (i, 0)),
                  full(conv_w), full(gate_p), full(qg), full(kg), full(bd)],
        out_specs=[tok(DN_QK), tok(DN_QK), tok(D_DN), tok(LANES), tok(D_SWA), tok(2 * D_KV), tok(2 * D_KV),
                   tok(D_DN)],
        out_shape=[jax.ShapeDtypeStruct((n_batch, t, DN_QK), F32),
                   jax.ShapeDtypeStruct((n_batch, t, DN_QK), F32),
                   jax.ShapeDtypeStruct((n_batch, t, D_DN), F32),
                   jax.ShapeDtypeStruct((n_batch, t, LANES), F32),
                   jax.ShapeDtypeStruct((n_batch, t, D_SWA), BF16),
                   jax.ShapeDtypeStruct((n_batch, t, 2 * D_KV), BF16),
                   jax.ShapeDtypeStruct((n_batch, t, 2 * D_KV), BF16),
                   jax.ShapeDtypeStruct((n_batch, t, D_DN), F32)],
        compiler_params=_cparams(("parallel", "parallel")),
        name="inprep",
    )(xu, xu, xu, mod, g1, w_pad, cos_t, sin_t, conv_w, gate_p, qg, kg, bd)


def _dn_dir_setup(q_ref, k_ref, v_ref, g_ref, tri, row0, rev, b):
    c = DN_CHUNK
    r = pl.ds(pl.multiple_of(row0, c), c)
    gb = g_ref[b, r, :]
    hi = gb.astype(BF16)
    r1 = gb - hi.astype(F32)
    mid = r1.astype(BF16)
    lo = (r1 - mid.astype(F32)).astype(BF16)
    gc = (jnp.dot(tri, hi, preferred_element_type=F32) + jnp.dot(tri, mid, preferred_element_type=F32)
          + jnp.dot(tri, lo, preferred_element_type=F32))
    last = 0 if rev else c - 1
    gl = gc[last:last + 1, :]
    ii = lax.broadcasted_iota(jnp.int32, (c, LANES), 0)
    jj = lax.broadcasted_iota(jnp.int32, (c, LANES), 1) & (c - 1)
    return dict(r=r, q=q_ref[b, r, :], k=k_ref[b, r, :], v=v_ref[b, r, :], gb=gb, gc=gc, gct=gc.T,
                eg=jnp.exp(gc), kdf=jnp.exp(gl - gc), egl=jnp.exp(gl),
                incl=(ii <= jj) if rev else (ii >= jj), strict=(ii < jj) if rev else (ii > jj),
                eye=(ii == jj).astype(F32))


def _dn_pair(fwd_refs, bwd_refs, of_ref, ob_ref, s_ref, tri_ref, row_f, row_b, n_batch):
    c = DN_CHUNK
    dirs = []
    for b in range(n_batch):
        dirs += [_dn_dir_setup(*fwd_refs, tri_ref[0], row_f, False, b),
                 _dn_dir_setup(*bwd_refs, tri_ref[1], row_b, True, b)]
    npair = DN_HEADS // 2
    chains = [(d, p) for d in range(2 * n_batch) for p in range(npair)]
    dot = functools.partial(jnp.dot, preferred_element_type=F32)
    lane = lax.broadcasted_iota(jnp.int32, (1, LANES), 1)
    low = lane < DN_DK
    same_head = ((lax.broadcasted_iota(jnp.int32, (LANES, LANES), 0) < c)
                 == (lax.broadcasted_iota(jnp.int32, (LANES, LANES), 1) < DN_DK))

    def bdiag(m):
        return jnp.where(same_head, jnp.concatenate([m, m], axis=0), 0.0).astype(BF16)

    def tile(name, d, p):
        return dirs[d][name][:, p * LANES:(p + 1) * LANES]

    def col(name, d, p, off=0):
        ln = off + (d % 2) * DN_HEADS + 2 * p
        a = dirs[d][name]
        return jnp.where(low, a[:, ln:ln + 1], a[:, ln + 1:ln + 2])

    kp = [tile("k", d, p) for d, p in chains]
    qp = [tile("q", d, p) for d, p in chains]
    vp = [tile("v", d, p) for d, p in chains]
    beta = [col("gb", d, p, 2 * DN_HEADS) for d, p in chains]
    egc = [col("eg", d, p) for d, p in chains]
    decay = []
    for d, p in chains:
        ln = (d % 2) * DN_HEADS + 2 * p
        gct = dirs[d]["gct"]
        grow = jnp.concatenate([gct[ln:ln + 1, :], gct[ln + 1:ln + 2, :]], axis=1)
        incl = dirs[d]["incl"]
        diff = col("gc", d, p) - grow
        decay.append(jnp.where(incl, jnp.exp(jnp.where(incl, diff, 0.0)), 0.0))
    kbp = [k * b for k, b in zip(kp, beta)]
    kq = [lax.dot_general(jnp.concatenate([kb, q], axis=0).astype(BF16), bdiag(k),
                          (((1,), (1,)), ((), ())), preferred_element_type=F32)
          for kb, q, k in zip(kbp, qp, kp)]
    lmat = [jnp.where(dirs[d]["strict"], a[:c] * dc, 0.0) for (d, p), a, dc in zip(chains, kq, decay)]
    aqk = [a[c:] * dc for a, dc in zip(kq, decay)]
    ii = lax.broadcasted_iota(jnp.int32, (c, LANES), 0)
    jj = lax.broadcasted_iota(jnp.int32, (c, LANES), 1) & (c - 1)
    eye = dirs[0]["eye"]
    base = DN_INV_BASE_LOG2
    assert base == 3
    ld = [jnp.where((ii >> base) == (jj >> base), m, 0.0) for m in lmat]
    pw = [dot(m.astype(BF16), bdiag(m)) for m in ld]
    r = [dot(jnp.concatenate([eye - m, pi], axis=0).astype(BF16), bdiag(pi)) for m, pi in zip(ld, pw)]
    x = [eye - m + ri[:c] for m, ri in zip(ld, r)]
    minv = [xi + dot(xi.astype(BF16), bdiag(ri[c:])) for xi, ri in zip(x, r)]
    for lb in range(base, 6):
        joined = ((ii >> (lb + 1)) == (jj >> (lb + 1))) & (((ii >> lb) & 1) != ((jj >> lb) & 1))
        mc = [dot(mi.astype(BF16), bdiag(jnp.where(joined, m, 0.0))) for mi, m in zip(minv, lmat)]
        minv = [mi - dot(mci.astype(BF16), bdiag(mi)) for mi, mci in zip(minv, mc)]
    tinv = [m.astype(BF16) for m in minv]
    u = [dot(ti, bdiag(v * b)) for ti, v, b in zip(tinv, vp, beta)]
    w = [dot(ti, bdiag(kb * e)) for ti, kb, e in zip(tinv, kbp, egc)]
    kdt = [(k * col("kdf", d, p)).T.astype(BF16) for (d, p), k in zip(chains, kp)]
    s = [s_ref[d, p] for d, p in chains]
    r2 = [dot(jnp.concatenate([wi, q * e], axis=0).astype(BF16), bdiag(si))
          for wi, q, e, si in zip(w, qp, egc, s)]
    vnew = [ui - ri[:c] for ui, ri in zip(u, r2)]
    r3 = [dot(a.astype(BF16), bdiag(vn)) for a, vn in zip(aqk, vnew)]
    sup = [dot(kt, vn.astype(BF16)) for kt, vn in zip(kdt, vnew)]
    for n, (d, p) in enumerate(chains):
        s_ref[d, p] = s[n] * col("egl", d, p) + jnp.where(low, sup[n][:c], sup[n][c:])
    outs = [ri[c:] + r3i for ri, r3i in zip(r2, r3)]
    for b in range(n_batch):
        lo = 2 * b * npair
        of_ref[b, dirs[2 * b]["r"], :] = jnp.concatenate(outs[lo:lo + npair], axis=1)
        ob_ref[b, dirs[2 * b + 1]["r"], :] = jnp.concatenate(outs[lo + npair:lo + 2 * npair], axis=1)


def _dn_body(qf, kf, vf, gf, qb, kb, vb, gbk, tri_ref, of_ref, ob_ref, s_ref, *, nchunk, n_batch):
    i = pl.program_id(0)

    @pl.when(i == 0)
    def _():
        s_ref[...] = jnp.zeros_like(s_ref)

    def step(cidx, carry):
        _dn_pair((qf, kf, vf, gf), (qb, kb, vb, gbk), of_ref, ob_ref, s_ref, tri_ref,
                 cidx * DN_CHUNK, (nchunk - 1 - cidx) * DN_CHUNK, n_batch)
        return carry

    lax.fori_loop(0, nchunk, step, 0)


def _deltanet(dq, dk, dv, gb, tri, ctx):
    n_batch, t, _ = dq.shape
    tb = TOK_TILE
    nblk = t // tb
    nctx = ctx // tb

    def bmap(i):
        return jnp.where(i < nctx, nctx - 1 - i, nblk - 1 - (i - nctx))

    fwd = lambda w: pl.BlockSpec((n_batch, tb, w), lambda i: (0, i, 0))
    bwd = lambda w: pl.BlockSpec((n_batch, tb, w), lambda i: (0, bmap(i), 0))
    body = functools.partial(_dn_body, nchunk=tb // DN_CHUNK, n_batch=n_batch)
    return pl.pallas_call(
        body,
        grid=(nblk,),
        in_specs=[fwd(DN_QK), fwd(DN_QK), fwd(D_DN), fwd(LANES),
                  bwd(DN_QK), bwd(DN_QK), bwd(D_DN), bwd(LANES),
                  pl.BlockSpec(tri.shape, lambda i: (0, 0, 0))],
        out_specs=[fwd(D_DN), bwd(D_DN)],
        out_shape=[jax.ShapeDtypeStruct((n_batch, t, D_DN), F32)] * 2,
        scratch_shapes=[pltpu.VMEM((2 * n_batch, DN_HEADS // 2, DN_DK, 2 * DN_DV), F32)],
        compiler_params=_cparams(("arbitrary",)),
        name="deltanet",
    )(dq, dk, dv, gb, dq, dk, dv, gb, tri)


def _swa_body(sink_ref, q_ref, kp, kc, kn, vp, vc, vn, kx, vx, o_ref, *, ctx, t):
    i = pl.program_id(1)
    bl = SWA_BLOCK
    nband = 3 * bl
    ncol = nband + ctx
    q = q_ref[0]
    row2 = lax.broadcasted_iota(jnp.int32, (2 * bl, 1), 0)
    qpos = i * bl + (row2 & (bl - 1))
    col = lax.broadcasted_iota(jnp.int32, (1, ncol), 1)
    kpos = (i - 1) * bl + col
    band_ok = ((jnp.abs(kpos - qpos) <= SWA_WINDOW) & (kpos >= ctx) & (kpos < t) & (qpos >= ctx)
               & (col < nband))
    mask = band_ok | (col >= nband)
    k2 = jnp.concatenate([kp[0], kc[0], kn[0], kx[0]], axis=0)
    v2 = jnp.concatenate([vp[0], vc[0], vn[0], vx[0]], axis=0)
    low = lax.broadcasted_iota(jnp.int32, (ncol, LANES), 1) < SWA_HD
    zero = jnp.zeros((ncol, LANES), BF16)
    pick = lambda a: [jnp.where(low, a[:, :LANES], zero), jnp.where(low, zero, a[:, LANES:]),
                      jnp.where(low, a[:, LANES:], zero), jnp.where(low, zero, a[:, :LANES])]
    kvar = pick(k2)
    vvar = pick(v2)
    pairs = D_SWA // LANES
    qg = [jnp.concatenate([q[:, (2 * g) * LANES:(2 * g + 1) * LANES],
                           q[:, (2 * g + 1) * LANES:(2 * g + 2) * LANES]], axis=0) for g in range(pairs // 2)]
    cases = [(g, par) for g in range(pairs // 2) for par in range(2)]
    scores = [lax.dot_general(qg[g], kvar[2 * g + par], (((1,), (1,)), ((), ())), preferred_element_type=F32)
              for g, par in cases]
    probs, rden = [], []
    for (g, par), s in zip(cases, scores):
        h_top = 4 * g + par
        sink = jnp.where(row2 < bl, sink_ref[h_top], sink_ref[h_top + 2])
        s = jnp.where(mask, s, NEG_INF)
        m = jnp.maximum(jnp.max(s, axis=-1, keepdims=True), sink)
        p = jnp.exp(s - m)
        rden.append(1.0 / (jnp.sum(p, axis=-1, keepdims=True) + jnp.exp(sink - m)))
        probs.append(p.astype(BF16))
    pv = [jnp.dot(p, vvar[2 * g + par], preferred_element_type=F32) * r
          for (g, par), p, r in zip(cases, probs, rden)]
    outs = []
    for g in range(pairs // 2):
        og = pv[2 * g] + pv[2 * g + 1]
        outs += [og[:bl], og[bl:]]
    o_ref[0] = jnp.concatenate(outs, axis=1).astype(BF16)


def _swa(sq, sk, sv, sinks, ctx):
    n_batch, t, _ = sq.shape
    bl = SWA_BLOCK
    nb = t // bl
    body = functools.partial(_swa_body, ctx=ctx, t=t)
    kvspec = lambda f: pl.BlockSpec((1, bl, 2 * D_KV), f)
    prev = lambda b, i: (b, jnp.maximum(i - 1, 0), 0)
    cur = lambda b, i: (b, i, 0)
    nxt = lambda b, i: (b, jnp.minimum(i + 1, nb - 1), 0)
    cspec = pl.BlockSpec((1, ctx, 2 * D_KV), lambda b, i: (b, 0, 0))
    return pl.pallas_call(
        body,
        grid=(n_batch, nb),
        in_specs=[pl.BlockSpec(memory_space=pltpu.SMEM),
                  pl.BlockSpec((1, bl, D_SWA), cur),
                  kvspec(prev), kvspec(cur), kvspec(nxt),
                  kvspec(prev), kvspec(cur), kvspec(nxt),
                  cspec, cspec],
        out_specs=pl.BlockSpec((1, bl, D_SWA), cur),
        out_shape=jax.ShapeDtypeStruct((n_batch, t, D_SWA), BF16),
        compiler_params=_cparams(("parallel", "parallel")),
        name="swa",
    )(sinks, sq, sk, sk, sk, sv, sv, sv, sk, sv)


def _post_body(of_ref, ob_ref, z_ref, sw_ref, x_ref, mod_ref, og_ref, wo_ref, g2_ref, rw_ref, rb_ref,
               bd_ref, tri_ref, xo_ref, h_ref, route_ref, cnt_ref, *, tm, ctx, n_batch, d):
    b = pl.program_id(0)
    i = pl.program_id(1)
    rows = i * tm + lax.broadcasted_iota(jnp.int32, (tm, 1), 0)
    is_ctx = rows < ctx
    o = of_ref[0] + ob_ref[0]
    ms = _head_sum(o * o, bd_ref[...]) * (1.0 / DN_DV)
    z = z_ref[0]
    dn = o * lax.rsqrt(ms + NORM_EPS) * og_ref[...] * (z * _sigmoid(z))
    mix = jnp.concatenate([dn.astype(BF16), sw_ref[0]], axis=1)
    proj = jnp.dot(mix, wo_ref[...], preferred_element_type=F32)
    gt1 = _mod_chunk(mod_ref, b, n_batch, is_ctx, 2, d)
    x = x_ref[0] + gt1 * proj
    xo_ref[0] = x

    ms2 = jnp.mean(x * x, axis=-1, keepdims=True)
    y = x * lax.rsqrt(ms2 + NORM_EPS) * g2_ref[...]
    sh2 = _mod_chunk(mod_ref, b, n_batch, is_ctx, 3, d)
    sc2 = _mod_chunk(mod_ref, b, n_batch, is_ctx, 4, d)
    h = y * (1.0 + sc2) + sh2
    h_ref[0] = h

    h_hi = h.astype(BF16)
    h_lo = (h - h_hi.astype(F32)).astype(BF16)
    logits = (jnp.dot(h_hi, rw_ref[0], preferred_element_type=F32)
              + jnp.dot(h_lo, rw_ref[0], preferred_element_type=F32)
              + jnp.dot(h_hi, rw_ref[1], preferred_element_type=F32)
              + jnp.dot(h_lo, rw_ref[1], preferred_element_type=F32)) + rb_ref[...]
    lane = lax.broadcasted_iota(jnp.int32, (1, LANES), 1)
    vals, idxs = [], []
    l = logits
    for _ in range(TOP_K):
        m = jnp.max(l, axis=-1, keepdims=True)
        idx = jnp.min(jnp.where(l == m, lane, LANES), axis=-1, keepdims=True)
        vals.append(m)
        idxs.append(idx)
        l = jnp.where(lane == idx, -3e38, l)
    es = [jnp.exp(v - vals[0]) for v in vals]
    den = es[0] + es[1] + es[2] + es[3]
    gates = [e / den for e in es]

    @pl.when((b == 0) & (i == 0))
    def _():
        cnt_ref[...] = jnp.zeros_like(cnt_ref)

    onehot = jnp.zeros((tm, LANES), F32)
    for idx in idxs:
        onehot = onehot + (lane == idx).astype(F32)
    run = cnt_ref[0:1, :]
    cum = jnp.dot(tri_ref[...], onehot.astype(BF16), preferred_element_type=F32) + run
    ranks = [jnp.sum(jnp.where(lane == idx, cum, 0.0), axis=-1, keepdims=True) for idx in idxs]
    cnt_ref[...] = jnp.broadcast_to(run + jnp.sum(onehot, axis=0, keepdims=True), cnt_ref.shape)
    route = jnp.zeros((tm, LANES), F32)
    for k in range(TOP_K):
        route = jnp.where(lane == k, idxs[k].astype(F32), route)
        route = jnp.where(lane == TOP_K + k, gates[k], route)
        route = jnp.where(lane == 2 * TOP_K + k, ranks[k], route)
    route_ref[0] = route


def _post(o_f, o_b, pz, o_sw, xu, mod, og, w_out, g2, rw, rb, bd, tri, ctx):
    n_batch, t, d = xu.shape
    tm = TOK_TILE
    body = functools.partial(_post_body, tm=tm, ctx=ctx, n_batch=n_batch, d=d)
    tok = lambda w: pl.BlockSpec((1, tm, w), lambda b, i: (b, i, 0))
    full = lambda a: pl.BlockSpec(a.shape, lambda b, i: (0,) * a.ndim)
    return pl.pallas_call(
        body,
        grid=(n_batch, t // tm),
        in_specs=[tok(D_DN), tok(D_DN), tok(D_DN), tok(D_SWA), tok(d),
                  full(mod), full(og), full(w_out), full(g2), full(rw), full(rb), full(bd), full(tri)],
        out_specs=[tok(d), tok(d), tok(LANES), pl.BlockSpec((SUBLANES, LANES), lambda b, i: (0, 0))],
        out_shape=[jax.ShapeDtypeStruct((n_batch, t, d), F32),
                   jax.ShapeDtypeStruct((n_batch, t, d), F32),
                   jax.ShapeDtypeStruct((n_batch, t, LANES), F32),
                   jax.ShapeDtypeStruct((SUBLANES, LANES), F32)],
        compiler_params=_cparams(("arbitrary", "arbitrary")),
        name="post_attn",
    )(o_f, o_b, pz, o_sw, xu, mod, og, w_out, g2, rw, rb, bd, tri)


def _dispatch_body(dest_ref, h_ref, xs_ref, sem, *, td):
    def issue(n, carry):
        for u in range(ROW_DMA_UNROLL):
            tk = n * ROW_DMA_UNROLL + u
            for k in range(TOP_K):
                dst = dest_ref[0, 0, tk * TOP_K + k]
                pltpu.make_async_copy(h_ref.at[pl.ds(tk, 1)], xs_ref.at[pl.ds(dst, 1)], sem).start(priority=k % 2)
        return carry

    lax.fori_loop(0, td // ROW_DMA_UNROLL, issue, 0)

    def drain(n, carry):
        for _ in range(ROW_DMA_UNROLL * TOP_K):
            pltpu.make_async_copy(h_ref.at[pl.ds(0, 1)], xs_ref.at[pl.ds(0, 1)], sem).wait()
        return carry

    lax.fori_loop(0, td // ROW_DMA_UNROLL, drain, 0)


def _dispatch(h_flat, dest, n_slots):
    ntok, d = h_flat.shape
    td = TOK_TILE
    body = functools.partial(_dispatch_body, td=td)
    return pl.pallas_call(
        body,
        grid=(ntok // td,),
        in_specs=[pl.BlockSpec((1, 1, td * TOP_K), lambda i: (i, 0, 0), memory_space=pltpu.SMEM),
                  pl.BlockSpec((td, d), lambda i: (i, 0))],
        out_specs=pl.BlockSpec(memory_space=pl.ANY),
        out_shape=jax.ShapeDtypeStruct((n_slots, d), F32),
        scratch_shapes=[pltpu.SemaphoreType.DMA(())],
        compiler_params=_cparams(("arbitrary",)),
        name="moe_dispatch",
    )(dest.reshape(ntok // td, 1, td * TOP_K), h_flat)


def _cast_rows(src_ref, dst_ref, rows_per_step):
    def step(n, carry):
        r = pl.ds(pl.multiple_of(n * rows_per_step, rows_per_step), rows_per_step)
        dst_ref[r, :] = src_ref[0, 0, r, :].astype(BF16)
        return carry

    lax.fori_loop(0, src_ref.shape[2] // rows_per_step, step, 0)


def _moe_body(te_ref, nv_ref, nu_ref, xs_ref, wgu_ref, bgu_ref, wdn_ref, bdn_ref, ys_ref, wgu_bf, wdn_bf, *, tm, de):
    j = pl.program_id(0)
    used = j < nu_ref[0]
    prev = te_ref[jnp.maximum(j - 1, 0)]

    @pl.when(used & ((j == 0) | (te_ref[j] != prev)))
    def _():
        _cast_rows(wgu_ref, wgu_bf, LANES)
        _cast_rows(wdn_ref, wdn_bf, LANES)

    @pl.when(used)
    def _():
        rows = lax.broadcasted_iota(jnp.int32, (tm, 1), 0)
        x = jnp.where(rows < nv_ref[j], xs_ref[...], 0.0).astype(BF16)
        gu = jnp.dot(x, wgu_bf[...], preferred_element_type=F32) + bgu_ref[0, 0]
        gate = jnp.minimum(gu[:, :de], SWIGLU_LIMIT)
        up = jnp.clip(gu[:, de:], -SWIGLU_LIMIT, SWIGLU_LIMIT)
        act = (up + 1.0) * gate * _sigmoid(SWIGLU_ALPHA * gate)
        ys_ref[...] = jnp.dot(act.astype(BF16), wdn_bf[...], preferred_element_type=F32) + bdn_ref[0, 0]

    @pl.when(jnp.logical_not(used))
    def _():
        ys_ref[...] = jnp.zeros_like(ys_ref)


def _moe(xs, tile_expert, tile_valid, n_used, wgu, bgu, wdn, bdn, layer):
    n_slots, d = xs.shape
    tm = MOE_TILE
    _, n_exp, _, de2 = wgu.shape
    de = de2 // 2
    body = functools.partial(_moe_body, tm=tm, de=de)
    wmap = lambda j, te, nv, nu: (layer, te[j], 0, 0)
    grid_spec = pltpu.PrefetchScalarGridSpec(
        num_scalar_prefetch=3,
        grid=(n_slots // tm,),
        in_specs=[pl.BlockSpec((tm, d), lambda j, te, nv, nu: (j, 0)),
                  pl.BlockSpec((1, 1, d, de2), wmap),
                  pl.BlockSpec((1, 1, 1, de2), wmap),
                  pl.BlockSpec((1, 1, de, d), wmap),
                  pl.BlockSpec((1, 1, 1, d), wmap)],
        out_specs=pl.BlockSpec((tm, d), lambda j, te, nv, nu: (j, 0)),
        scratch_shapes=[pltpu.VMEM((d, de2), BF16), pltpu.VMEM((de, d), BF16)],
    )
    depth = wgu.shape[0]
    return pl.pallas_call(
        body,
        grid_spec=grid_spec,
        out_shape=jax.ShapeDtypeStruct((n_slots, d), F32),
        compiler_params=_cparams(("arbitrary",), MOE_VMEM_LIMIT),
        name="moe_experts",
    )(tile_expert, tile_valid, n_used, xs, wgu, bgu.reshape(depth, n_exp, 1, de2), wdn,
      bdn.reshape(depth, n_exp, 1, d))


def _combine_body(dest_ref, ys_ref, x_ref, route_ref, mod_ref, o_ref, buf, sem, *, tc, ctx, n_batch, d, first):
    b = pl.program_id(0)
    i = pl.program_id(1) + first

    def issue(n, carry):
        for u in range(ROW_DMA_UNROLL):
            tk = n * ROW_DMA_UNROLL + u
            for k in range(TOP_K):
                src = dest_ref[0, 0, tk * TOP_K + k]
                pltpu.make_async_copy(ys_ref.at[pl.ds(src, 1)], buf.at[k, pl.ds(tk, 1)], sem).start(priority=k % 2)
        return carry

    lax.fori_loop(0, tc // ROW_DMA_UNROLL, issue, 0)

    def drain(n, carry):
        for _ in range(ROW_DMA_UNROLL * TOP_K):
            pltpu.make_async_copy(ys_ref.at[pl.ds(0, 1)], buf.at[0, pl.ds(0, 1)], sem).wait()
        return carry

    lax.fori_loop(0, tc // ROW_DMA_UNROLL, drain, 0)

    route = route_ref[0]
    y = None
    for k in range(TOP_K):
        term = route[:, TOP_K + k:TOP_K + k + 1] * buf[k]
        y = term if y is None else y + term
    rows = i * tc + lax.broadcasted_iota(jnp.int32, (tc, 1), 0)
    gt2 = _mod_chunk(mod_ref, b, n_batch, rows < ctx, 5, d)
    o_ref[0] = x_ref[0] + gt2 * y


def _combine(dest, ys, xu, route, mod, ctx, latent_only):
    n_batch, t, d = xu.shape
    tc = TOK_TILE
    nt = t // tc
    first = ctx // tc if latent_only else 0
    body = functools.partial(_combine_body, tc=tc, ctx=ctx, n_batch=n_batch, d=d, first=first)
    return pl.pallas_call(
        body,
        grid=(n_batch, nt - first),
        in_specs=[pl.BlockSpec((1, 1, tc * TOP_K), lambda b, i: (b * nt + i + first, 0, 0),
                               memory_space=pltpu.SMEM),
                  pl.BlockSpec(memory_space=pl.ANY),
                  pl.BlockSpec((1, tc, d), lambda b, i: (b, i + first, 0)),
                  pl.BlockSpec((1, tc, LANES), lambda b, i: (b, i + first, 0)),
                  pl.BlockSpec(mod.shape, lambda b, i: (0, 0))],
        out_specs=pl.BlockSpec((1, tc, d), lambda b, i: (b, i, 0)),
        out_shape=jax.ShapeDtypeStruct((n_batch, t - first * tc, d), F32),
        scratch_shapes=[pltpu.VMEM((TOP_K, tc, d), F32), pltpu.SemaphoreType.DMA(())],
        compiler_params=_cparams(("arbitrary", "arbitrary")),
        name="moe_combine",
    )(dest.reshape(n_batch * nt, 1, tc * TOP_K), ys, xu, route, mod)


def _rope_tables(seq, ctx):
    rows = seq // GRID_W
    row = jnp.repeat(jnp.arange(rows, dtype=F32), GRID_W)
    col = jnp.tile(jnp.arange(GRID_W, dtype=F32), rows)
    inv_freq = jnp.power(ROPE_THETA, -jnp.arange(ROPE_FREQS, dtype=F32) / ROPE_FREQS)
    ang_r = row[:, None] * inv_freq
    ang_c = col[:, None] * inv_freq
    cos = jnp.concatenate([jnp.cos(ang_r), jnp.cos(ang_r), jnp.cos(ang_c), jnp.cos(ang_c)], axis=1)
    sin = jnp.concatenate([-jnp.sin(ang_r), jnp.sin(ang_r), -jnp.sin(ang_c), jnp.sin(ang_c)], axis=1)
    cos = jnp.concatenate([jnp.ones((ctx, SWA_HD), F32), cos], axis=0)
    sin = jnp.concatenate([jnp.zeros((ctx, SWA_HD), F32), sin], axis=0)
    reps = LANES // SWA_HD
    return jnp.tile(cos, (1, reps)), jnp.tile(sin, (1, reps))


def _pad_w_in(w_in):
    sizes = (DN_QK, DN_QK, D_DN, D_DN, 2 * DN_HEADS, 2 * DN_HEADS, D_SWA, D_KV, D_KV)
    offs = np.concatenate([[0], np.cumsum(sizes)])
    part = lambda n: w_in[:, offs[n]:offs[n + 1]]
    pad = jnp.zeros((w_in.shape[0], LANES - 4 * DN_HEADS), w_in.dtype)
    return jnp.concatenate([part(0), part(1), part(2), part(3), part(6), part(7), part(8), part(4), part(5), pad],
                           axis=1).astype(BF16)


def _lane_row(v, fill=0.0):
    out = jnp.full((1, LANES), fill, F32)
    return out.at[0, :v.shape[0]].set(v.astype(F32))


def kernel(x, c, ctx, c_ctx, ada_w, ada_b, norm1_g, w_in, dn_conv_w, dn_a_log, dn_dt_bias, dn_out_g, q_norm_g,
           k_norm_g, sinks, w_out, norm2_g, router_w, router_b, w_gate_up, b_gate_up, w_down, b_down):
    n_batch, seq, d = x.shape
    n_ctx = ctx.shape[1]
    depth = ada_w.shape[0]
    t = n_ctx + seq
    ntok = n_batch * t
    assert n_batch < SUBLANES and n_ctx % TOK_TILE == 0 and seq % TOK_TILE == 0

    xu = jnp.concatenate([ctx, x], axis=1)
    cvec = jnp.zeros((SUBLANES, d), F32).at[:n_batch].set(c).at[n_batch].set(c_ctx)
    cos_t, sin_t = _rope_tables(seq, n_ctx)
    hid = np.arange(DN_QK) // DN_DK
    bd = jnp.asarray(hid[:, None] == hid[None, :], BF16)
    cidx = np.arange(DN_CHUNK)
    tri_dn = jnp.asarray(np.stack([cidx[:, None] >= cidx[None, :], cidx[:, None] <= cidx[None, :]]), BF16)
    ridx = np.arange(TOK_TILE)
    tri_rank = jnp.asarray(ridx[:, None] > ridx[None, :], BF16)

    n_slots = ntok * TOP_K + N_EXPERTS * MOE_TILE
    n_tiles = n_slots // MOE_TILE
    eid = jnp.arange(N_EXPERTS, dtype=jnp.int32)

    for l in range(depth):
        mod = _ada(cvec, ada_w, ada_b, l)
        conv_w = jnp.zeros((SUBLANES, dn_conv_w.shape[2]), F32).at[:DN_CONV].set(dn_conv_w[l])
        gate_p = jnp.concatenate([_lane_row(dn_a_log[l].reshape(-1)), _lane_row(dn_dt_bias[l].reshape(-1)),
                                  jnp.zeros((SUBLANES - 2, LANES), F32)], axis=0)
        qg = jnp.tile(q_norm_g[l], SWA_Q_HEADS).reshape(1, D_SWA)
        kg = jnp.tile(k_norm_g[l], SWA_KV_HEADS).reshape(1, D_KV)
        dq, dk, dv, gb, sq, sk, sv, pz = _inprep(xu, mod, norm1_g[l].reshape(1, d), _pad_w_in(w_in[l]), cos_t, sin_t,
                                                 conv_w, gate_p, qg, kg, bd, n_ctx)
        o_f, o_b = _deltanet(dq, dk, dv, gb, tri_dn, n_ctx)
        o_sw = _swa(sq, sk, sv, sinks[l], n_ctx)
        og = jnp.tile(dn_out_g[l], DN_HEADS).reshape(1, D_DN)
        rw32 = jnp.zeros((d, LANES), F32).at[:, :N_EXPERTS].set(router_w[l])
        rw_hi = rw32.astype(BF16)
        rw = jnp.stack([rw_hi, (rw32 - rw_hi.astype(F32)).astype(BF16)])
        rb = _lane_row(router_b[l], NEG_INF)
        xu, h, route, cnt = _post(o_f, o_b, pz, o_sw, xu, mod, og, w_out[l].astype(BF16),
                                  norm2_g[l].reshape(1, d), rw, rb, bd, tri_rank, n_ctx)

        route_flat = route.reshape(ntok, LANES)
        idx = route_flat[:, :TOP_K].astype(jnp.int32)
        rank = route_flat[:, 2 * TOP_K:3 * TOP_K].astype(jnp.int32)
        counts = cnt[0, :N_EXPERTS].astype(jnp.int32)
        padded = (counts + MOE_TILE - 1) // MOE_TILE * MOE_TILE
        pad_end = jnp.cumsum(padded)
        pad_start = pad_end - padded
        dest = jnp.sum(jnp.where(idx[:, :, None] == eid[None, None, :], pad_start[None, None, :], 0), axis=-1) + rank
        tile_lo = jnp.arange(n_tiles, dtype=jnp.int32) * MOE_TILE
        tile_expert = jnp.minimum(jnp.sum(pad_end[None, :] <= tile_lo[:, None], axis=1), N_EXPERTS - 1).astype(jnp.int32)
        tile_valid = jnp.clip((pad_start + counts)[tile_expert] - tile_lo, 0, MOE_TILE).astype(jnp.int32)
        n_used = (pad_end[-1:] // MOE_TILE).astype(jnp.int32)

        xs = _dispatch(h.reshape(ntok, d), dest, n_slots)
        ys = _moe(xs, tile_expert, tile_valid, n_used, w_gate_up, b_gate_up, w_down, b_down, l)
        xu = _combine(dest, ys, xu, route, mod, n_ctx, latent_only=(l == depth - 1))

    return xu
```

```python
import functools
import math

import jax
import jax.numpy as jnp
import numpy as np
from jax import lax
from jax.experimental import pallas as pl
from jax.experimental.pallas import tpu as pltpu

F32 = jnp.float32
BF16 = jnp.bfloat16
HIGHEST = lax.Precision.HIGHEST

GRID_W = 64
DN_HEADS = 8
DN_DK = 64
DN_DV = 64
DN_CONV = 5
DN_CHUNK = 64
SWA_Q_HEADS = 8
SWA_KV_HEADS = 2
SWA_HD = 64
SWA_WINDOW = 128
SWA_BLOCK = 128
ROPE_THETA = 10000.0
ROPE_FREQS = SWA_HD // 4
DN_QK = DN_HEADS * DN_DK
D_DN = DN_HEADS * DN_DV
D_SWA = SWA_Q_HEADS * SWA_HD
D_KV = SWA_KV_HEADS * SWA_HD
N_EXPERTS = 32
TOP_K = 4
SWIGLU_LIMIT = 7.0
SWIGLU_ALPHA = 1.702
NORM_EPS = 1e-6
NEG_INF = -1e30

LANES = 128
SUBLANES = 8
TOK_TILE = 256
MOE_TILE = 512
DISPATCH_TILE = 512
DN_INV_BASE_LOG2 = 3
ROW_DMA_UNROLL = 4
VMEM_LIMIT = 48 * 1024 * 1024
MOE_VMEM_LIMIT = 52 * 1024 * 1024


def _sigmoid(x):
    return 1.0 / (1.0 + jnp.exp(-x))


def _cparams(sem, vmem=None):
    return pltpu.CompilerParams(dimension_semantics=sem, vmem_limit_bytes=vmem or VMEM_LIMIT)


def _head_sum(x2, bd):
    hi = x2.astype(BF16)
    lo = (x2 - hi.astype(F32)).astype(BF16)
    return (jnp.dot(hi, bd, preferred_element_type=F32) + jnp.dot(lo, bd, preferred_element_type=F32))


def _mod_chunk(mod_ref, b, n_batch, is_ctx, k, d):
    mb = mod_ref[pl.ds(b, 1), k * d:(k + 1) * d]
    mc = mod_ref[n_batch:n_batch + 1, k * d:(k + 1) * d]
    return jnp.where(is_ctx, mc, mb)


def _ada_body(c_ref, w_ref, b_ref, o_ref):
    c = c_ref[...]
    s = c * _sigmoid(c)
    o_ref[...] = jnp.dot(s, w_ref[0], precision=HIGHEST, preferred_element_type=F32) + b_ref[0]


def _ada(cvec, ada_w, ada_b, layer):
    depth, d, n = ada_w.shape
    tn = n // 4
    return pl.pallas_call(
        _ada_body,
        grid=(n // tn,),
        in_specs=[pl.BlockSpec((SUBLANES, d), lambda j: (0, 0)),
                  pl.BlockSpec((1, d, tn), lambda j: (layer, 0, j)),
                  pl.BlockSpec((1, 1, tn), lambda j: (layer, 0, j))],
        out_specs=pl.BlockSpec((SUBLANES, tn), lambda j: (0, j)),
        out_shape=jax.ShapeDtypeStruct((SUBLANES, n), F32),
        compiler_params=_cparams(("arbitrary",)),
        name="ada_mod",
    )(cvec, ada_w, ada_b.reshape(depth, 1, n))


def _swap16(x):
    lane = lax.broadcasted_iota(jnp.int32, (1, LANES), 1)
    even = ((lane >> 4) & 1) == 0
    nxt = pltpu.roll(x, LANES - 16, 1)
    prv = pltpu.roll(x, 16, 1)
    return jnp.where(even, nxt, prv)


def _rope(x, cos, sin):
    outs = []
    for c in range(x.shape[1] // LANES):
        xc = x[:, c * LANES:(c + 1) * LANES]
        outs.append(xc * cos + _swap16(xc) * sin)
    return outs[0] if len(outs) == 1 else jnp.concatenate(outs, axis=1)


def _inprep_body(x_ref, xp_ref, xn_ref, mod_ref, g_ref, w_ref, cos_ref, sin_ref, cw_ref, gp_ref,
                 qg_ref, kg_ref, bd_ref,
                 oq, ok, ov, ogb, osq, osk, osv, oz, *, tm, ctx, t, n_batch, d):
    b = pl.program_id(0)
    i = pl.program_id(1)
    lo = i * tm
    hi = lo + tm
    xe = jnp.concatenate([xp_ref[0], x_ref[0], xn_ref[0]], axis=0)
    ms = jnp.mean(xe * xe, axis=-1, keepdims=True)
    y = xe * lax.rsqrt(ms + NORM_EPS) * g_ref[...]
    is_ctx = lo < ctx
    sh = _mod_chunk(mod_ref, b, n_batch, is_ctx, 0, d)
    sc = _mod_chunk(mod_ref, b, n_batch, is_ctx, 1, d)
    h = (y * (1.0 + sc) + sh).astype(BF16)
    p = jnp.dot(h, w_ref[...], preferred_element_type=F32)
    c0 = 2 * DN_QK + D_DN
    c1 = c0 + D_DN
    c2 = c1 + D_SWA
    c3 = c2 + 2 * D_KV
    own = slice(SUBLANES, SUBLANES + tm)
    oz[0] = p[own, c0:c1]
    prv = jnp.where((lo == 0) | (lo == ctx), 0.0, p[:SUBLANES, :c0])
    nxt = jnp.where((hi == ctx) | (hi == t), 0.0, p[SUBLANES + tm:, :c0])
    xe = jnp.concatenate([prv, p[own, :c0], nxt], axis=0)
    pad = DN_CONV // 2
    acc = None
    for j in range(DN_CONV):
        term = xe[SUBLANES - pad + j:SUBLANES - pad + j + tm] * cw_ref[j:j + 1, :]
        acc = term if acc is None else acc + term
    y = acc * _sigmoid(acc)
    bd = bd_ref[...]
    q = y[:, :DN_QK]
    k = y[:, DN_QK:2 * DN_QK]
    oq[0] = q * lax.rsqrt(_head_sum(q * q, bd) + NORM_EPS) * (DN_DK ** -0.5)
    ok[0] = k * lax.rsqrt(_head_sum(k * k, bd) + NORM_EPS)
    ov[0] = y[:, 2 * DN_QK:]

    ab = p[own, c3:c3 + LANES]
    lane = lax.broadcasted_iota(jnp.int32, (1, LANES), 1)
    z = ab + gp_ref[1:2, :]
    softplus = jnp.maximum(z, 0.0) + jnp.log(1.0 + jnp.exp(-jnp.abs(z)))
    g = -jnp.exp(gp_ref[0:1, :]) * softplus
    ogb[0] = jnp.where(lane < 2 * DN_HEADS, g, _sigmoid(ab))

    cos = cos_ref[...]
    sin = sin_ref[...]
    sq = p[own, c1:c2]
    qn = sq * lax.rsqrt(_head_sum(sq * sq, bd) * (1.0 / SWA_HD) + NORM_EPS) * qg_ref[...]
    osq[0] = (_rope(qn, cos, sin) * (SWA_HD ** -0.5)).astype(BF16)
    skv = p[own, c2:c3]
    sk = skv[:, :D_KV]
    kn = sk * lax.rsqrt(_head_sum(sk * sk, bd[:D_KV, :D_KV]) * (1.0 / SWA_HD) + NORM_EPS) * kg_ref[...]
    kr = _rope(kn, cos, sin)
    osk[0] = jnp.concatenate([kr, pltpu.roll(kr, SWA_HD, 1)], axis=1).astype(BF16)
    sv = skv[:, D_KV:]
    osv[0] = jnp.concatenate([sv, pltpu.roll(sv, SWA_HD, 1)], axis=1).astype(BF16)


def _inprep(xu, mod, g1, w_pad, cos_t, sin_t, conv_w, gate_p, qg, kg, bd, ctx):
    n_batch, t, d = xu.shape
    tm = TOK_TILE
    hb = tm // SUBLANES
    nh = t // SUBLANES
    body = functools.partial(_inprep_body, tm=tm, ctx=ctx, t=t, n_batch=n_batch, d=d)
    tok = lambda w: pl.BlockSpec((1, tm, w), lambda b, i: (b, i, 0))
    full = lambda a: pl.BlockSpec(a.shape, lambda b, i: (0,) * a.ndim)
    return pl.pallas_call(
        body,
        grid=(n_batch, t // tm),
        in_specs=[tok(d),
                  pl.BlockSpec((1, SUBLANES, d), lambda b, i: (b, jnp.maximum(i * hb - 1, 0), 0)),
                  pl.BlockSpec((1, SUBLANES, d), lambda b, i: (b, jnp.minimum((i + 1) * hb, nh - 1), 0)),
                  full(mod), full(g1), full(w_pad),
                  pl.BlockSpec((tm, LANES), lambda b, i: (i, 0)),
                  pl.BlockSpec((tm, LANES), lambda b, i: (i, 0)),
                  full(conv_w), full(gate_p), full(qg), full(kg), full(bd)],
        out_specs=[tok(DN_QK), tok(DN_QK), tok(D_DN), tok(LANES), tok(D_SWA), tok(2 * D_KV), tok(2 * D_KV),
                   tok(D_DN)],
        out_shape=[jax.ShapeDtypeStruct((n_batch, t, DN_QK), F32),
                   jax.ShapeDtypeStruct((n_batch, t, DN_QK), F32),
                   jax.ShapeDtypeStruct((n_batch, t, D_DN), F32),
                   jax.ShapeDtypeStruct((n_batch, t, LANES), F32),
                   jax.ShapeDtypeStruct((n_batch, t, D_SWA), BF16),
                   jax.ShapeDtypeStruct((n_batch, t, 2 * D_KV), BF16),
                   jax.ShapeDtypeStruct((n_batch, t, 2 * D_KV), BF16),
                   jax.ShapeDtypeStruct((n_batch, t, D_DN), F32)],
        compiler_params=_cparams(("parallel", "parallel")),
        name="inprep",
    )(xu, xu, xu, mod, g1, w_pad, cos_t, sin_t, conv_w, gate_p, qg, kg, bd)


def _dn_dir_setup(q_ref, k_ref, v_ref, g_ref, tri, row0, rev, b):
    c = DN_CHUNK
    r = pl.ds(pl.multiple_of(row0, c), c)
    gb = g_ref[b, r, :]
    hi = gb.astype(BF16)
    r1 = gb - hi.astype(F32)
    mid = r1.astype(BF16)
    lo = (r1 - mid.astype(F32)).astype(BF16)
    gc = (jnp.dot(tri, hi, preferred_element_type=F32) + jnp.dot(tri, mid, preferred_element_type=F32)
          + jnp.dot(tri, lo, preferred_element_type=F32))
    last = 0 if rev else c - 1
    gl = gc[last:last + 1, :]
    ii = lax.broadcasted_iota(jnp.int32, (c, LANES), 0)
    jj = lax.broadcasted_iota(jnp.int32, (c, LANES), 1) & (c - 1)
    return dict(r=r, q=q_ref[b, r, :], k=k_ref[b, r, :], v=v_ref[b, r, :], gb=gb, gc=gc, gct=gc.T,
                eg=jnp.exp(gc), kdf=jnp.exp(gl - gc), egl=jnp.exp(gl),
                incl=(ii <= jj) if rev else (ii >= jj), strict=(ii < jj) if rev else (ii > jj),
                eye=(ii == jj).astype(F32))


def _dn_pair(fwd_refs, bwd_refs, of_ref, ob_ref, s_ref, tri_ref, row_f, row_b, n_batch):
    c = DN_CHUNK
    dirs = []
    for b in range(n_batch):
        dirs += [_dn_dir_setup(*fwd_refs, tri_ref[0], row_f, False, b),
                 _dn_dir_setup(*bwd_refs, tri_ref[1], row_b, True, b)]
    npair = DN_HEADS // 2
    chains = [(d, p) for d in range(2 * n_batch) for p in range(npair)]
    dot = functools.partial(jnp.dot, preferred_element_type=F32)
    lane = lax.broadcasted_iota(jnp.int32, (1, LANES), 1)
    low = lane < DN_DK
    same_head = ((lax.broadcasted_iota(jnp.int32, (LANES, LANES), 0) < c)
                 == (lax.broadcasted_iota(jnp.int32, (LANES, LANES), 1) < DN_DK))

    def bdiag(m):
        return jnp.where(same_head, jnp.concatenate([m, m], axis=0), 0.0).astype(BF16)

    def tile(name, d, p):
        return dirs[d][name][:, p * LANES:(p + 1) * LANES]

    def col(name, d, p, off=0):
        ln = off + (d % 2) * DN_HEADS + 2 * p
        a = dirs[d][name]
        return jnp.where(low, a[:, ln:ln + 1], a[:, ln + 1:ln + 2])

    kp = [tile("k", d, p) for d, p in chains]
    qp = [tile("q", d, p) for d, p in chains]
    vp = [tile("v", d, p) for d, p in chains]
    beta = [col("gb", d, p, 2 * DN_HEADS) for d, p in chains]
    egc = [col("eg", d, p) for d, p in chains]
    decay = []
    for d, p in chains:
        ln = (d % 2) * DN_HEADS + 2 * p
        gct = dirs[d]["gct"]
        grow = jnp.concatenate([gct[ln:ln + 1, :], gct[ln + 1:ln + 2, :]], axis=1)
        incl = dirs[d]["incl"]
        diff = col("gc", d, p) - grow
        decay.append(jnp.where(incl, jnp.exp(jnp.where(incl, diff, 0.0)), 0.0))
    kbp = [k * b for k, b in zip(kp, beta)]
    kq = [lax.dot_general(jnp.concatenate([kb, q], axis=0).astype(BF16), bdiag(k),
                          (((1,), (1,)), ((), ())), preferred_element_type=F32)
          for kb, q, k in zip(kbp, qp, kp)]
    lmat = [jnp.where(dirs[d]["strict"], a[:c] * dc, 0.0) for (d, p), a, dc in zip(chains, kq, decay)]
    aqk = [a[c:] * dc for a, dc in zip(kq, decay)]
    ii = lax.broadcasted_iota(jnp.int32, (c, LANES), 0)
    jj = lax.broadcasted_iota(jnp.int32, (c, LANES), 1) & (c - 1)
    eye = dirs[0]["eye"]
    base = DN_INV_BASE_LOG2
    assert base == 3
    ld = [jnp.where((ii >> base) == (jj >> base), m, 0.0) for m in lmat]
    pw = [dot(m.astype(BF16), bdiag(m)) for m in ld]
    r = [dot(jnp.concatenate([eye - m, pi], axis=0).astype(BF16), bdiag(pi)) for m, pi in zip(ld, pw)]
    x = [eye - m + ri[:c] for m, ri in zip(ld, r)]
    minv = [xi + dot(xi.astype(BF16), bdiag(ri[c:])) for xi, ri in zip(x, r)]
    for lb in range(base, 6):
        joined = ((ii >> (lb + 1)) == (jj >> (lb + 1))) & (((ii >> lb) & 1) != ((jj >> lb) & 1))
        mc = [dot(mi.astype(BF16), bdiag(jnp.where(joined, m, 0.0))) for mi, m in zip(minv, lmat)]
        minv = [mi - dot(mci.astype(BF16), bdiag(mi)) for mi, mci in zip(minv, mc)]
    tinv = [m.astype(BF16) for m in minv]
    u = [dot(ti, bdiag(v * b)) for ti, v, b in zip(tinv, vp, beta)]
    w = [dot(ti, bdiag(kb * e)) for ti, kb, e in zip(tinv, kbp, egc)]
    kdt = [(k * col("kdf", d, p)).T.astype(BF16) for (d, p), k in zip(chains, kp)]
    s = [s_ref[d, p] for d, p in chains]
    r2 = [dot(jnp.concatenate([wi, q * e], axis=0).astype(BF16), bdiag(si))
          for wi, q, e, si in zip(w, qp, egc, s)]
    vnew = [ui - ri[:c] for ui, ri in zip(u, r2)]
    r3 = [dot(a.astype(BF16), bdiag(vn)) for a, vn in zip(aqk, vnew)]
    sup = [dot(kt, vn.astype(BF16)) for kt, vn in zip(kdt, vnew)]
    for n, (d, p) in enumerate(chains):
        s_ref[d, p] = s[n] * col("egl", d, p) + jnp.where(low, sup[n][:c], sup[n][c:])
    outs = [ri[c:] + r3i for ri, r3i in zip(r2, r3)]
    for b in range(n_batch):
        lo = 2 * b * npair
        of_ref[b, dirs[2 * b]["r"], :] = jnp.concatenate(outs[lo:lo + npair], axis=1)
        ob_ref[b, dirs[2 * b + 1]["r"], :] = jnp.concatenate(outs[lo + npair:lo + 2 * npair], axis=1)


def _dn_body(qf, kf, vf, gf, qb, kb, vb, gbk, tri_ref, of_ref, ob_ref, s_ref, *, nchunk, n_batch):
    i = pl.program_id(0)

    @pl.when(i == 0)
    def _():
        s_ref[...] = jnp.zeros_like(s_ref)

    def step(cidx, carry):
        _dn_pair((qf, kf, vf, gf), (qb, kb, vb, gbk), of_ref, ob_ref, s_ref, tri_ref,
                 cidx * DN_CHUNK, (nchunk - 1 - cidx) * DN_CHUNK, n_batch)
        return carry

    lax.fori_loop(0, nchunk, step, 0)


def _deltanet(dq, dk, dv, gb, tri, ctx):
    n_batch, t, _ = dq.shape
    tb = TOK_TILE
    nblk = t // tb
    nctx = ctx // tb

    def bmap(i):
        return jnp.where(i < nctx, nctx - 1 - i, nblk - 1 - (i - nctx))

    fwd = lambda w: pl.BlockSpec((n_batch, tb, w), lambda i: (0, i, 0))
    bwd = lambda w: pl.BlockSpec((n_batch, tb, w), lambda i: (0, bmap(i), 0))
    body = functools.partial(_dn_body, nchunk=tb // DN_CHUNK, n_batch=n_batch)
    return pl.pallas_call(
        body,
        grid=(nblk,),
        in_specs=[fwd(DN_QK), fwd(DN_QK), fwd(D_DN), fwd(LANES),
                  bwd(DN_QK), bwd(DN_QK), bwd(D_DN), bwd(LANES),
                  pl.BlockSpec(tri.shape, lambda i: (0, 0, 0))],
        out_specs=[fwd(D_DN), bwd(D_DN)],
        out_shape=[jax.ShapeDtypeStruct((n_batch, t, D_DN), F32)] * 2,
        scratch_shapes=[pltpu.VMEM((2 * n_batch, DN_HEADS // 2, DN_DK, 2 * DN_DV), F32)],
        compiler_params=_cparams(("arbitrary",)),
        name="deltanet",
    )(dq, dk, dv, gb, dq, dk, dv, gb, tri)


def _swa_body(sink_ref, q_ref, kp, kc, kn, vp, vc, vn, kx, vx, o_ref, *, ctx, t):
    i = pl.program_id(1)
    bl = SWA_BLOCK
    nband = 3 * bl
    ncol = nband + ctx
    q = q_ref[0]
    row2 = lax.broadcasted_iota(jnp.int32, (2 * bl, 1), 0)
    qpos = i * bl + (row2 & (bl - 1))
    col = lax.broadcasted_iota(jnp.int32, (1, ncol), 1)
    kpos = (i - 1) * bl + col
    band_ok = ((jnp.abs(kpos - qpos) <= SWA_WINDOW) & (kpos >= ctx) & (kpos < t) & (qpos >= ctx)
               & (col < nband))
    mask = band_ok | (col >= nband)
    k2 = jnp.concatenate([kp[0], kc[0], kn[0], kx[0]], axis=0)
    v2 = jnp.concatenate([vp[0], vc[0], vn[0], vx[0]], axis=0)
    low = lax.broadcasted_iota(jnp.int32, (ncol, LANES), 1) < SWA_HD
    zero = jnp.zeros((ncol, LANES), BF16)
    pick = lambda a: [jnp.where(low, a[:, :LANES], zero), jnp.where(low, zero, a[:, LANES:]),
                      jnp.where(low, a[:, LANES:], zero), jnp.where(low, zero, a[:, :LANES])]
    kvar = pick(k2)
    vvar = pick(v2)
    pairs = D_SWA // LANES
    qg = [jnp.concatenate([q[:, (2 * g) * LANES:(2 * g + 1) * LANES],
                           q[:, (2 * g + 1) * LANES:(2 * g + 2) * LANES]], axis=0) for g in range(pairs // 2)]
    cases = [(g, par) for g in range(pairs // 2) for par in range(2)]
    scores = [lax.dot_general(qg[g], kvar[2 * g + par], (((1,), (1,)), ((), ())), preferred_element_type=F32)
              for g, par in cases]
    probs, rden = [], []
    for (g, par), s in zip(cases, scores):
        h_top = 4 * g + par
        sink = jnp.where(row2 < bl, sink_ref[h_top], sink_ref[h_top + 2])
        s = jnp.where(mask, s, NEG_INF)
        m = jnp.maximum(jnp.max(s, axis=-1, keepdims=True), sink)
        p = jnp.exp(s - m)
        rden.append(1.0 / (jnp.sum(p, axis=-1, keepdims=True) + jnp.exp(sink - m)))
        probs.append(p.astype(BF16))
    pv = [jnp.dot(p, vvar[2 * g + par], preferred_element_type=F32) * r
          for (g, par), p, r in zip(cases, probs, rden)]
    outs = []
    for g in range(pairs // 2):
        og = pv[2 * g] + pv[2 * g + 1]
        outs += [og[:bl], og[bl:]]
    o_ref[0] = jnp.concatenate(outs, axis=1).astype(BF16)


def _swa(sq, sk, sv, sinks, ctx):
    n_batch, t, _ = sq.shape
    bl = SWA_BLOCK
    nb = t // bl
    body = functools.partial(_swa_body, ctx=ctx, t=t)
    kvspec = lambda f: pl.BlockSpec((1, bl, 2 * D_KV), f)
    prev = lambda b, i: (b, jnp.maximum(i - 1, 0), 0)
    cur = lambda b, i: (b, i, 0)
    nxt = lambda b, i: (b, jnp.minimum(i + 1, nb - 1), 0)
    cspec = pl.BlockSpec((1, ctx, 2 * D_KV), lambda b, i: (b, 0, 0))
    return pl.pallas_call(
        body,
        grid=(n_batch, nb),
        in_specs=[pl.BlockSpec(memory_space=pltpu.SMEM),
                  pl.BlockSpec((1, bl, D_SWA), cur),
                  kvspec(prev), kvspec(cur), kvspec(nxt),
                  kvspec(prev), kvspec(cur), kvspec(nxt),
                  cspec, cspec],
        out_specs=pl.BlockSpec((1, bl, D_SWA), cur),
        out_shape=jax.ShapeDtypeStruct((n_batch, t, D_SWA), BF16),
        compiler_params=_cparams(("parallel", "parallel")),
        name="swa",
    )(sinks, sq, sk, sk, sk, sv, sv, sv, sk, sv)


def _post_body(of_ref, ob_ref, z_ref, sw_ref, x_ref, mod_ref, og_ref, wo_ref, g2_ref, rw_ref, rb_ref,
               bd_ref, tri_ref, xo_ref, h_ref, route_ref, cnt_ref, *, tm, ctx, n_batch, d):
    b = pl.program_id(0)
    i = pl.program_id(1)
    rows = i * tm + lax.broadcasted_iota(jnp.int32, (tm, 1), 0)
    is_ctx = rows < ctx
    o = of_ref[0] + ob_ref[0]
    ms = _head_sum(o * o, bd_ref[...]) * (1.0 / DN_DV)
    z = z_ref[0]
    dn = o * lax.rsqrt(ms + NORM_EPS) * og_ref[...] * (z * _sigmoid(z))
    mix = jnp.concatenate([dn.astype(BF16), sw_ref[0]], axis=1)
    proj = jnp.dot(mix, wo_ref[...], preferred_element_type=F32)
    gt1 = _mod_chunk(mod_ref, b, n_batch, is_ctx, 2, d)
    x = x_ref[0] + gt1 * proj
    xo_ref[0] = x

    ms2 = jnp.mean(x * x, axis=-1, keepdims=True)
    y = x * lax.rsqrt(ms2 + NORM_EPS) * g2_ref[...]
    sh2 = _mod_chunk(mod_ref, b, n_batch, is_ctx, 3, d)
    sc2 = _mod_chunk(mod_ref, b, n_batch, is_ctx, 4, d)
    h = y * (1.0 + sc2) + sh2
    h_ref[0] = h

    h_hi = h.astype(BF16)
    h_lo = (h - h_hi.astype(F32)).astype(BF16)
    logits = (jnp.dot(h_hi, rw_ref[0], preferred_element_type=F32)
              + jnp.dot(h_lo, rw_ref[0], preferred_element_type=F32)
              + jnp.dot(h_hi, rw_ref[1], preferred_element_type=F32)
              + jnp.dot(h_lo, rw_ref[1], preferred_element_type=F32)) + rb_ref[...]
    lane = lax.broadcasted_iota(jnp.int32, (1, LANES), 1)
    vals, idxs = [], []
    l = logits
    for _ in range(TOP_K):
        m = jnp.max(l, axis=-1, keepdims=True)
        idx = jnp.min(jnp.where(l == m, lane, LANES), axis=-1, keepdims=True)
        vals.append(m)
        idxs.append(idx)
        l = jnp.where(lane == idx, -3e38, l)
    es = [jnp.exp(v - vals[0]) for v in vals]
    den = es[0] + es[1] + es[2] + es[3]
    gates = [e / den for e in es]

    @pl.when((b == 0) & (i == 0))
    def _():
        cnt_ref[...] = jnp.zeros_like(cnt_ref)

    onehot = jnp.zeros((tm, LANES), F32)
    for idx in idxs:
        onehot = onehot + (lane == idx).astype(F32)
    run = cnt_ref[0:1, :]
    cum = jnp.dot(tri_ref[...], onehot.astype(BF16), preferred_element_type=F32) + run
    ranks = [jnp.sum(jnp.where(lane == idx, cum, 0.0), axis=-1, keepdims=True) for idx in idxs]
    cnt_ref[...] = jnp.broadcast_to(run + jnp.sum(onehot, axis=0, keepdims=True), cnt_ref.shape)
    route = jnp.zeros((tm, LANES), F32)
    for k in range(TOP_K):
        route = jnp.where(lane == k, idxs[k].astype(F32), route)
        route = jnp.where(lane == TOP_K + k, gates[k], route)
        route = jnp.where(lane == 2 * TOP_K + k, ranks[k], route)
    route_ref[0] = route


def _post(o_f, o_b, pz, o_sw, xu, mod, og, w_out, g2, rw, rb, bd, tri, ctx):
    n_batch, t, d = xu.shape
    tm = TOK_TILE
    body = functools.partial(_post_body, tm=tm, ctx=ctx, n_batch=n_batch, d=d)
    tok = lambda w: pl.BlockSpec((1, tm, w), lambda b, i: (b, i, 0))
    full = lambda a: pl.BlockSpec(a.shape, lambda b, i: (0,) * a.ndim)
    return pl.pallas_call(
        body,
        grid=(n_batch, t // tm),
        in_specs=[tok(D_DN), tok(D_DN), tok(D_DN), tok(D_SWA), tok(d),
                  full(mod), full(og), full(w_out), full(g2), full(rw), full(rb), full(bd), full(tri)],
        out_specs=[tok(d), tok(d), tok(LANES), pl.BlockSpec((SUBLANES, LANES), lambda b, i: (0, 0))],
        out_shape=[jax.ShapeDtypeStruct((n_batch, t, d), F32),
                   jax.ShapeDtypeStruct((n_batch, t, d), F32),
                   jax.ShapeDtypeStruct((n_batch, t, LANES), F32),
                   jax.ShapeDtypeStruct((SUBLANES, LANES), F32)],
        compiler_params=_cparams(("arbitrary", "arbitrary")),
        name="post_attn",
    )(o_f, o_b, pz, o_sw, xu, mod, og, w_out, g2, rw, rb, bd, tri)


def _dispatch_body(dest_ref, h_ref, xs_ref, sem, *, td):
    def issue(n, carry):
        for u in range(ROW_DMA_UNROLL):
            tk = n * ROW_DMA_UNROLL + u
            for k in range(TOP_K):
                dst = dest_ref[0, 0, tk * TOP_K + k]
                pltpu.make_async_copy(h_ref.at[pl.ds(tk, 1)], xs_ref.at[pl.ds(dst, 1)], sem).start(priority=k % 2)
        return carry

    lax.fori_loop(0, td // ROW_DMA_UNROLL, issue, 0)

    def drain(n, carry):
        for _ in range(ROW_DMA_UNROLL * TOP_K):
            pltpu.make_async_copy(h_ref.at[pl.ds(0, 1)], xs_ref.at[pl.ds(0, 1)], sem).wait()
        return carry

    lax.fori_loop(0, td // ROW_DMA_UNROLL, drain, 0)


def _dispatch(h_flat, dest, n_slots):
    ntok, d = h_flat.shape
    td = DISPATCH_TILE
    body = functools.partial(_dispatch_body, td=td)
    return pl.pallas_call(
        body,
        grid=(ntok // td,),
        in_specs=[pl.BlockSpec((1, 1, td * TOP_K), lambda i: (i, 0, 0), memory_space=pltpu.SMEM),
                  pl.BlockSpec((td, d), lambda i: (i, 0))],
        out_specs=pl.BlockSpec(memory_space=pl.ANY),
        out_shape=jax.ShapeDtypeStruct((n_slots, d), F32),
        scratch_shapes=[pltpu.SemaphoreType.DMA(())],
        compiler_params=_cparams(("arbitrary",)),
        name="moe_dispatch",
    )(dest.reshape(ntok // td, 1, td * TOP_K), h_flat)


def _cast_rows(src_ref, dst_ref, rows_per_step):
    def step(n, carry):
        r = pl.ds(pl.multiple_of(n * rows_per_step, rows_per_step), rows_per_step)
        dst_ref[r, :] = src_ref[0, 0, r, :].astype(BF16)
        return carry

    lax.fori_loop(0, src_ref.shape[2] // rows_per_step, step, 0)


def _moe_body(te_ref, nv_ref, nu_ref, xs_ref, wgu_ref, bgu_ref, wdn_ref, bdn_ref, ys_ref, wgu_bf, wdn_bf, *, tm, de):
    j = pl.program_id(0)
    used = j < nu_ref[0]
    prev = te_ref[jnp.maximum(j - 1, 0)]

    @pl.when(used & ((j == 0) | (te_ref[j] != prev)))
    def _():
        _cast_rows(wgu_ref, wgu_bf, LANES)
        _cast_rows(wdn_ref, wdn_bf, LANES)

    @pl.when(used)
    def _():
        rows = lax.broadcasted_iota(jnp.int32, (tm, 1), 0)
        x = jnp.where(rows < nv_ref[j], xs_ref[...], 0.0).astype(BF16)
        gu = jnp.dot(x, wgu_bf[...], preferred_element_type=F32) + bgu_ref[0, 0]
        gate = jnp.minimum(gu[:, :de], SWIGLU_LIMIT)
        up = jnp.clip(gu[:, de:], -SWIGLU_LIMIT, SWIGLU_LIMIT)
        act = (up + 1.0) * gate * _sigmoid(SWIGLU_ALPHA * gate)
        ys_ref[...] = jnp.dot(act.astype(BF16), wdn_bf[...], preferred_element_type=F32) + bdn_ref[0, 0]

    @pl.when(jnp.logical_not(used))
    def _():
        ys_ref[...] = jnp.zeros_like(ys_ref)


def _moe(xs, tile_expert, tile_valid, n_used, wgu, bgu, wdn, bdn, layer):
    n_slots, d = xs.shape
    tm = MOE_TILE
    _, n_exp, _, de2 = wgu.shape
    de = de2 // 2
    body = functools.partial(_moe_body, tm=tm, de=de)
    wmap = lambda j, te, nv, nu: (layer, te[j], 0, 0)
    grid_spec = pltpu.PrefetchScalarGridSpec(
        num_scalar_prefetch=3,
        grid=(n_slots // tm,),
        in_specs=[pl.BlockSpec((tm, d), lambda j, te, nv, nu: (j, 0)),
                  pl.BlockSpec((1, 1, d, de2), wmap),
                  pl.BlockSpec((1, 1, 1, de2), wmap),
                  pl.BlockSpec((1, 1, de, d), wmap),
                  pl.BlockSpec((1, 1, 1, d), wmap)],
        out_specs=pl.BlockSpec((tm, d), lambda j, te, nv, nu: (j, 0)),
        scratch_shapes=[pltpu.VMEM((d, de2), BF16), pltpu.VMEM((de, d), BF16)],
    )
    depth = wgu.shape[0]
    return pl.pallas_call(
        body,
        grid_spec=grid_spec,
        out_shape=jax.ShapeDtypeStruct((n_slots, d), F32),
        compiler_params=_cparams(("arbitrary",), MOE_VMEM_LIMIT),
        name="moe_experts",
    )(tile_expert, tile_valid, n_used, xs, wgu, bgu.reshape(depth, n_exp, 1, de2), wdn,
      bdn.reshape(depth, n_exp, 1, d))


def _combine_body(dest_ref, ys_ref, x_ref, route_ref, mod_ref, o_ref, buf, sem, *, tc, ctx, n_batch, d, first):
    b = pl.program_id(0)
    i = pl.program_id(1) + first

    def issue(n, carry):
        for u in range(ROW_DMA_UNROLL):
            tk = n * ROW_DMA_UNROLL + u
            for k in range(TOP_K):
                src = dest_ref[0, 0, tk * TOP_K + k]
                pltpu.make_async_copy(ys_ref.at[pl.ds(src, 1)], buf.at[k, pl.ds(tk, 1)], sem).start(priority=k % 2)
        return carry

    lax.fori_loop(0, tc // ROW_DMA_UNROLL, issue, 0)

    def drain(n, carry):
        for _ in range(ROW_DMA_UNROLL * TOP_K):
            pltpu.make_async_copy(ys_ref.at[pl.ds(0, 1)], buf.at[0, pl.ds(0, 1)], sem).wait()
        return carry

    lax.fori_loop(0, tc // ROW_DMA_UNROLL, drain, 0)

    route = route_ref[0]
    y = None
    for k in range(TOP_K):
        term = route[:, TOP_K + k:TOP_K + k + 1] * buf[k]
        y = term if y is None else y + term
    rows = i * tc + lax.broadcasted_iota(jnp.int32, (tc, 1), 0)
    gt2 = _mod_chunk(mod_ref, b, n_batch, rows < ctx, 5, d)
    o_ref[0] = x_ref[0] + gt2 * y


def _combine(dest, ys, xu, route, mod, ctx, latent_only):
    n_batch, t, d = xu.shape
    tc = TOK_TILE
    nt = t // tc
    first = ctx // tc if latent_only else 0
    body = functools.partial(_combine_body, tc=tc, ctx=ctx, n_batch=n_batch, d=d, first=first)
    return pl.pallas_call(
        body,
        grid=(n_batch, nt - first),
        in_specs=[pl.BlockSpec((1, 1, tc * TOP_K), lambda b, i: (b * nt + i + first, 0, 0),
                               memory_space=pltpu.SMEM),
                  pl.BlockSpec(memory_space=pl.ANY),
                  pl.BlockSpec((1, tc, d), lambda b, i: (b, i + first, 0)),
                  pl.BlockSpec((1, tc, LANES), lambda b, i: (b, i + first, 0)),
                  pl.BlockSpec(mod.shape, lambda b, i: (0, 0))],
        out_specs=pl.BlockSpec((1, tc, d), lambda b, i: (b, i, 0)),
        out_shape=jax.ShapeDtypeStruct((n_batch, t - first * tc, d), F32),
        scratch_shapes=[pltpu.VMEM((TOP_K, tc, d), F32), pltpu.SemaphoreType.DMA(())],
        compiler_params=_cparams(("arbitrary", "arbitrary")),
        name="moe_combine",
    )(dest.reshape(n_batch * nt, 1, tc * TOP_K), ys, xu, route, mod)


def _rope_tables(seq, ctx):
    rows = seq // GRID_W
    row = jnp.repeat(jnp.arange(rows, dtype=F32), GRID_W)
    col = jnp.tile(jnp.arange(GRID_W, dtype=F32), rows)
    inv_freq = jnp.power(ROPE_THETA, -jnp.arange(ROPE_FREQS, dtype=F32) / ROPE_FREQS)
    ang_r = row[:, None] * inv_freq
    ang_c = col[:, None] * inv_freq
    cos = jnp.concatenate([jnp.cos(ang_r), jnp.cos(ang_r), jnp.cos(ang_c), jnp.cos(ang_c)], axis=1)
    sin = jnp.concatenate([-jnp.sin(ang_r), jnp.sin(ang_r), -jnp.sin(ang_c), jnp.sin(ang_c)], axis=1)
    cos = jnp.concatenate([jnp.ones((ctx, SWA_HD), F32), cos], axis=0)
    sin = jnp.concatenate([jnp.zeros((ctx, SWA_HD), F32), sin], axis=0)
    reps = LANES // SWA_HD
    return jnp.tile(cos, (1, reps)), jnp.tile(sin, (1, reps))


def _pad_w_in(w_in):
    sizes = (DN_QK, DN_QK, D_DN, D_DN, 2 * DN_HEADS, 2 * DN_HEADS, D_SWA, D_KV, D_KV)
    offs = np.concatenate([[0], np.cumsum(sizes)])
    part = lambda n: w_in[:, offs[n]:offs[n + 1]]
    pad = jnp.zeros((w_in.shape[0], LANES - 4 * DN_HEADS), w_in.dtype)
    return jnp.concatenate([part(0), part(1), part(2), part(3), part(6), part(7), part(8), part(4), part(5), pad],
                           axis=1).astype(BF16)


def _lane_row(v, fill=0.0):
    out = jnp.full((1, LANES), fill, F32)
    return out.at[0, :v.shape[0]].set(v.astype(F32))


def kernel(x, c, ctx, c_ctx, ada_w, ada_b, norm1_g, w_in, dn_conv_w, dn_a_log, dn_dt_bias, dn_out_g, q_norm_g,
           k_norm_g, sinks, w_out, norm2_g, router_w, router_b, w_gate_up, b_gate_up, w_down, b_down):
    n_batch, seq, d = x.shape
    n_ctx = ctx.shape[1]
    depth = ada_w.shape[0]
    t = n_ctx + seq
    ntok = n_batch * t
    assert n_batch < SUBLANES and n_ctx % TOK_TILE == 0 and seq % TOK_TILE == 0 and ntok % DISPATCH_TILE == 0

    xu = jnp.concatenate([ctx, x], axis=1)
    cvec = jnp.zeros((SUBLANES, d), F32).at[:n_batch].set(c).at[n_batch].set(c_ctx)
    cos_t, sin_t = _rope_tables(seq, n_ctx)
    hid = np.arange(DN_QK) // DN_DK
    bd = jnp.asarray(hid[:, None] == hid[None, :], BF16)
    cidx = np.arange(DN_CHUNK)
    tri_dn = jnp.asarray(np.stack([cidx[:, None] >= cidx[None, :], cidx[:, None] <= cidx[None, :]]), BF16)
    ridx = np.arange(TOK_TILE)
    tri_rank = jnp.asarray(ridx[:, None] > ridx[None, :], BF16)

    n_slots = ntok * TOP_K + N_EXPERTS * MOE_TILE
    n_tiles = n_slots // MOE_TILE
    eid = jnp.arange(N_EXPERTS, dtype=jnp.int32)

    for l in range(depth):
        mod = _ada(cvec, ada_w, ada_b, l)
        conv_w = jnp.zeros((SUBLANES, dn_conv_w.shape[2]), F32).at[:DN_CONV].set(dn_conv_w[l])
        gate_p = jnp.concatenate([_lane_row(dn_a_log[l].reshape(-1)), _lane_row(dn_dt_bias[l].reshape(-1)),
                                  jnp.zeros((SUBLANES - 2, LANES), F32)], axis=0)
        qg = jnp.tile(q_norm_g[l], SWA_Q_HEADS).reshape(1, D_SWA)
        kg = jnp.tile(k_norm_g[l], SWA_KV_HEADS).reshape(1, D_KV)
        dq, dk, dv, gb, sq, sk, sv, pz = _inprep(xu, mod, norm1_g[l].reshape(1, d), _pad_w_in(w_in[l]), cos_t, sin_t,
                                                 conv_w, gate_p, qg, kg, bd, n_ctx)
        o_f, o_b = _deltanet(dq, dk, dv, gb, tri_dn, n_ctx)
        o_sw = _swa(sq, sk, sv, sinks[l], n_ctx)
        og = jnp.tile(dn_out_g[l], DN_HEADS).reshape(1, D_DN)
        rw32 = jnp.zeros((d, LANES), F32).at[:, :N_EXPERTS].set(router_w[l])
        rw_hi = rw32.astype(BF16)
        rw = jnp.stack([rw_hi, (rw32 - rw_hi.astype(F32)).astype(BF16)])
        rb = _lane_row(router_b[l], NEG_INF)
        xu, h, route, cnt = _post(o_f, o_b, pz, o_sw, xu, mod, og, w_out[l].astype(BF16),
                                  norm2_g[l].reshape(1, d), rw, rb, bd, tri_rank, n_ctx)

        route_flat = route.reshape(ntok, LANES)
        idx = route_flat[:, :TOP_K].astype(jnp.int32)
        rank = route_flat[:, 2 * TOP_K:3 * TOP_K].astype(jnp.int32)
        counts = cnt[0, :N_EXPERTS].astype(jnp.int32)
        padded = (counts + MOE_TILE - 1) // MOE_TILE * MOE_TILE
        pad_end = jnp.cumsum(padded)
        pad_start = pad_end - padded
        dest = jnp.sum(jnp.where(idx[:, :, None] == eid[None, None, :], pad_start[None, None, :], 0), axis=-1) + rank
        tile_lo = jnp.arange(n_tiles, dtype=jnp.int32) * MOE_TILE
        tile_expert = jnp.minimum(jnp.sum(pad_end[None, :] <= tile_lo[:, None], axis=1), N_EXPERTS - 1).astype(jnp.int32)
        tile_valid = jnp.clip((pad_start + counts)[tile_expert] - tile_lo, 0, MOE_TILE).astype(jnp.int32)
        n_used = (pad_end[-1:] // MOE_TILE).astype(jnp.int32)

        xs = _dispatch(h.reshape(ntok, d), dest, n_slots)
        ys = _moe(xs, tile_expert, tile_valid, n_used, w_gate_up, b_gate_up, w_down, b_down, l)
        xu = _combine(dest, ys, xu, route, mod, n_ctx, latent_only=(l == depth - 1))

    return xu
```
